```python
import math, functools
import jax, jax.numpy as jnp
from jax import lax
import numpy as np

D_MODEL = 4096
BATCH = 4
SEQ = 2048
DEPTH = 2
DEC_BATCH = 128
DEC_SEQ = 1
PAST_LEN = 16384
PAGE_SIZE = 128

MIX_WIDTH = D_MODEL
FOX_DIM = 64
FOX_HEADS = MIX_WIDTH // 4 // FOX_DIM
DIFF_QK = 64
DIFF_V = 2 * DIFF_QK
DIFF_HEADS = MIX_WIDTH // 4 // DIFF_V
MLA_NOPE = 128
MLA_ROPE = 64
MLA_V = 128
MLA_HEADS = MIX_WIDTH // 2 // MLA_V
MLA_QK = MLA_NOPE + MLA_ROPE
MLA_Q_RANK = 3 * D_MODEL // 16
MLA_KV_RANK = D_MODEL // 32
N_EXPERTS = 32
TOP_K = 4
D_EXPERT = D_MODEL // 8
SWIGLU_ALPHA = 1.702
SWIGLU_LIMIT = 7.0
ROPE_THETA = 10000.0
Q_BLOCK = 128
NORM_EPS = 1e-6
FORGET_BIAS = 2.0
ADA_STD = 0.5

FOX_ROW = 2 * FOX_DIM + FOX_HEADS
DIFF_ROW = 2 * DIFF_QK + DIFF_V
MLA_ROW = MLA_KV_RANK + MLA_ROPE
IN_SIZES = (FOX_HEADS * FOX_DIM, FOX_DIM, FOX_DIM, FOX_HEADS,
            DIFF_HEADS * 2 * DIFF_QK, 2 * DIFF_QK, DIFF_V,
            MLA_Q_RANK, MLA_KV_RANK, MLA_ROPE)
IN_COLS = sum(IN_SIZES)
IN_OFFSETS = tuple(sum(IN_SIZES[:i + 1]) for i in range(len(IN_SIZES) - 1))
FOX_SCALE = FOX_DIM ** -0.5
DIFF_SCALE = DIFF_QK ** -0.5
MLA_SCALE = MLA_QK ** -0.5

kernel_name = 'hybrid_fox_diff_mla_moe_step'


def rms_norm(x, g):
    xf = x.astype(jnp.float32)
    y = xf * lax.rsqrt(jnp.mean(jnp.square(xf), axis=-1, keepdims=True) + NORM_EPS)
    return (y * g.astype(jnp.float32)).astype(x.dtype)


def modulate(x, g, shift, scale):
    return rms_norm(x, g) * (1.0 + scale) + shift


def rope(x, pos):
    half = x.shape[-1] // 2
    inv_freq = ROPE_THETA ** (-jnp.arange(half, dtype=jnp.float32) / half)
    ang = pos.astype(jnp.float32)[:, None] * inv_freq[None, :]
    bshape = (1, pos.shape[0]) + (1,) * (x.ndim - 3) + (half,)
    cos, sin = jnp.cos(ang).reshape(bshape), jnp.sin(ang).reshape(bshape)
    xf = x.astype(jnp.float32)
    x1, x2 = xf[..., :half], xf[..., half:]
    return jnp.concatenate([x1 * cos - x2 * sin, x2 * cos + x1 * sin], axis=-1).astype(x.dtype)


def gather_pages(pool, page_table):
    g = pool[page_table]
    return g.reshape(g.shape[0], g.shape[1] * g.shape[2], g.shape[3])


def seg_scores(q, ks):
    return jnp.concatenate([jnp.einsum('bqhd,bkd->bhqk', q, k) for k in ks], axis=-1).astype(jnp.float32)


def seg_values(p, vs):
    out, start = None, 0
    for v in vs:
        n = v.shape[1]
        term = jnp.einsum('bhqk,bkd->bqhd', p[..., start:start + n].astype(v.dtype), v)
        out = term if out is None else out + term
        start += n
    return out


def masked_softmax(s, mask):
    return jax.nn.softmax(jnp.where(mask, s, -jnp.inf), axis=-1)


def diff_lambda(lam_vecs, lam_init):
    lv = lam_vecs.astype(jnp.float32)
    return jnp.exp(jnp.sum(lv[0] * lv[1])) - jnp.exp(jnp.sum(lv[2] * lv[3])) + lam_init


def project_mixers(h, pos, p):
    B, S, _ = h.shape
    z = jnp.einsum('bsd,dc->bsc', h, p['w_in'])
    fq, fk, fv, ff, dq, dk, dv, cq, ckv, kr = jnp.split(z, IN_OFFSETS, axis=-1)
    qf = rms_norm(fq.reshape(B, S, FOX_HEADS, FOX_DIM), p['fox_qn'])
    kf = rms_norm(fk, p['fox_kn'])
    logf = jax.nn.log_sigmoid(ff.astype(jnp.float32) + p['fox_fb'].astype(jnp.float32))
    qd = rope(rms_norm(dq.reshape(B, S, DIFF_HEADS, 2, DIFF_QK), p['diff_qn']), pos)
    kd = rope(rms_norm(dk.reshape(B, S, 2, DIFF_QK), p['diff_kn']), pos)
    q = jnp.einsum('bsr,rhe->bshe', rms_norm(cq, p['mla_cqn']), p['mla_wuq'])
    q = rms_norm(q, p['mla_qn'])
    q_lat = jnp.einsum('bshn,chn->bshc', q[..., :MLA_NOPE], p['mla_wuk'])
    qm = jnp.concatenate([q_lat, rope(q[..., MLA_NOPE:], pos)], axis=-1)
    ckv = rms_norm(ckv, p['mla_ckvn'])
    kr = rope(rms_norm(kr, p['mla_krn']), pos)
    rows = (jnp.concatenate([kf, fv, logf.astype(kf.dtype)], axis=-1),
            jnp.concatenate([kd.reshape(B, S, 2 * DIFF_QK), dv], axis=-1),
            jnp.concatenate([ckv, kr], axis=-1))
    return qf, logf, qd, qm, rows


def mixer_core(qf, cq, qd, qm, segs, ck, mask, lam):
    f_k = [s[0][..., :FOX_DIM] for s in segs]
    f_v = [s[0][..., FOX_DIM:2 * FOX_DIM] for s in segs]
    d_k1 = [s[1][..., :DIFF_QK] for s in segs]
    d_k2 = [s[1][..., DIFF_QK:2 * DIFF_QK] for s in segs]
    d_v = [s[1][..., 2 * DIFF_QK:] for s in segs]
    m_k = [s[2] for s in segs]
    m_v = [s[2][..., :MLA_KV_RANK] for s in segs]
    decay = jnp.swapaxes(cq, 1, 2)[..., :, None] - jnp.swapaxes(ck, 1, 2)[..., None, :]
    pf = masked_softmax(seg_scores(qf, f_k) * FOX_SCALE + decay, mask)
    of = seg_values(pf, f_v)
    p1 = masked_softmax(seg_scores(qd[:, :, :, 0], d_k1) * DIFF_SCALE, mask)
    p2 = masked_softmax(seg_scores(qd[:, :, :, 1], d_k2) * DIFF_SCALE, mask)
    od = seg_values(p1 - lam * p2, d_v)
    pm = masked_softmax(seg_scores(qm, m_k) * MLA_SCALE, mask)
    ol = seg_values(pm, m_v)
    return of, od, ol


def attend_prompt(qf, logf, qd, qm, rows, lam):
    B, S = qf.shape[:2]
    cum = lax.cumsum(logf, axis=1)
    k_pos = jnp.arange(S)

    def one_block(i):
        start = i * Q_BLOCK
        take = lambda a: lax.dynamic_slice_in_dim(a, start, Q_BLOCK, axis=1)
        mask = k_pos[None, :] <= (start + jnp.arange(Q_BLOCK))[:, None]
        return mixer_core(take(qf), take(cum), take(qd), take(qm), (rows,), cum, mask, lam)

    outs = lax.map(one_block, jnp.arange(S // Q_BLOCK))
    return tuple(jnp.moveaxis(o, 0, 1).reshape((B, S) + o.shape[3:]) for o in outs)


def attend_sample(past, qf, logf, qd, qm, rows, lam):
    Q, L = qf.shape[1], past[0].shape[1]
    cum_new = lax.cumsum(logf, axis=1)
    suffix = lax.cumsum(past[0][..., 2 * FOX_DIM:].astype(jnp.float32), axis=1, reverse=True)
    past_cum = -jnp.concatenate([suffix[:, 1:], jnp.zeros_like(suffix[:, :1])], axis=1)
    ck = jnp.concatenate([past_cum, cum_new], axis=1)
    mask = jnp.concatenate([jnp.ones((Q, L), bool), jnp.tril(jnp.ones((Q, Q), bool))], axis=1)
    return mixer_core(qf, cum_new, qd, qm, (past, rows), ck, mask, lam)


def merge_heads(of, od, ol, p, lam_init):
    B, S = of.shape[:2]
    od = rms_norm(od, p['diff_subln']) * (1.0 - lam_init)
    om = jnp.einsum('bshc,chv->bshv', ol, p['mla_wuv'])
    o = jnp.concatenate([of.reshape(B, S, -1), od.reshape(B, S, -1), om.reshape(B, S, -1)], axis=-1)
    return jnp.einsum('bsm,md->bsd', o, p['w_out'])


def moe(h, p):
    logits = (jnp.einsum('bsd,de->bse', h, p['router_w']) + p['router_b']).astype(jnp.float32)
    top_v, top_i = lax.top_k(logits, TOP_K)
    gates = jnp.einsum('bsk,bske->bse', jax.nn.softmax(top_v, axis=-1),
                       jax.nn.one_hot(top_i, N_EXPERTS, dtype=jnp.float32)).astype(h.dtype)
    gu = jnp.einsum('bsd,edf->bsef', h, p['moe_w_gu']) + p['moe_b_gu']
    g = jnp.minimum(gu[..., :D_EXPERT], SWIGLU_LIMIT)
    u = jnp.clip(gu[..., D_EXPERT:], -SWIGLU_LIMIT, SWIGLU_LIMIT)
    act = g * jax.nn.sigmoid(SWIGLU_ALPHA * g) * (u + 1.0) * gates[..., None]
    return (jnp.einsum('bsef,efd->bsd', act, p['moe_w_dn'])
            + jnp.einsum('bse,ed->bsd', gates, p['moe_b_dn']))


def trunk_layer(x, c, pos, p, lam_init, attend):
    mod = jnp.einsum('bc,cm->bm', jax.nn.silu(c), p['ada_w']) + p['ada_b']
    sh1, sc1, g1, sh2, sc2, g2 = (m[:, None, :] for m in jnp.split(mod, 6, axis=-1))
    qf, logf, qd, qm, rows = project_mixers(modulate(x, p['norm1_g'], sh1, sc1), pos, p)
    lam = diff_lambda(p['diff_lam'], lam_init)
    of, od, ol = attend(qf, logf, qd, qm, rows, lam)
    x = x + g1 * merge_heads(of, od, ol, p, lam_init)
    x = x + g2 * moe(modulate(x, p['norm2_g'], sh2, sc2), p)
    return x, rows


def setup_inputs(seed: int = 0) -> dict:
    key = jax.random.key(seed)
    ks = iter(jax.random.split(key, 48))
    nrm = lambda shape, std: std * jax.random.normal(next(ks), shape, jnp.float32)
    L, D, E, F = DEPTH, D_MODEL, N_EXPERTS, D_EXPERT
    n_pages = PAST_LEN // PAGE_SIZE
    n_used = DEC_BATCH * n_pages
    n_pool = n_used + max(1, n_used // 4)
    x_prompt = nrm((BATCH, SEQ, D), 1.0)
    x_sample = nrm((DEC_BATCH, DEC_SEQ, D), 1.0)
    fox_raw = nrm((L, n_pool, PAGE_SIZE, FOX_ROW), 1.0)
    cache_fox = jnp.concatenate([fox_raw[..., :2 * FOX_DIM],
                                 jax.nn.log_sigmoid(FORGET_BIAS + fox_raw[..., 2 * FOX_DIM:])], axis=-1)
    cache_diff = nrm((L, n_pool, PAGE_SIZE, DIFF_ROW), 1.0)
    cache_mla = nrm((L, n_pool, PAGE_SIZE, MLA_ROW), 1.0)
    page_table = jax.random.permutation(next(ks), n_pool)[:n_used].reshape(DEC_BATCH, n_pages).astype(jnp.int32)
    return {
        'x_prompt': x_prompt,
        'x_sample': x_sample,
        'cache_fox': cache_fox,
        'cache_diff': cache_diff,
        'cache_mla': cache_mla,
        'page_table': page_table,
        'c_prompt': nrm((BATCH, D), 1.0),
        'c_sample': nrm((DEC_BATCH, D), 1.0),
        'ada_w': nrm((L, D, 6 * D), ADA_STD * D ** -0.5),
        'ada_b': nrm((L, 6 * D), 0.02),
        'norm1_g': 1.0 + nrm((L, D), 0.02),
        'norm2_g': 1.0 + nrm((L, D), 0.02),
        'w_in': nrm((L, D, IN_COLS), D ** -0.5),
        'fox_fb': FORGET_BIAS + nrm((L, FOX_HEADS), 0.1),
        'fox_qn': 1.0 + nrm((L, FOX_DIM), 0.02),
        'fox_kn': 1.0 + nrm((L, FOX_DIM), 0.02),
        'diff_qn': 1.0 + nrm((L, DIFF_QK), 0.02),
        'diff_kn': 1.0 + nrm((L, DIFF_QK), 0.02),
        'diff_lam': nrm((L, 4, DIFF_QK), 0.1),
        'diff_subln': 1.0 + nrm((L, DIFF_V), 0.02),
        'mla_cqn': 1.0 + nrm((L, MLA_Q_RANK), 0.02),
        'mla_ckvn': 1.0 + nrm((L, MLA_KV_RANK), 0.02),
        'mla_krn': 1.0 + nrm((L, MLA_ROPE), 0.02),
        'mla_qn': 1.0 + nrm((L, MLA_QK), 0.02),
        'mla_wuq': nrm((L, MLA_Q_RANK, MLA_HEADS, MLA_QK), MLA_Q_RANK ** -0.5),
        'mla_wuk': nrm((L, MLA_KV_RANK, MLA_HEADS, MLA_NOPE), MLA_KV_RANK ** -0.5),
        'mla_wuv': nrm((L, MLA_KV_RANK, MLA_HEADS, MLA_V), MLA_KV_RANK ** -0.5),
        'w_out': nrm((L, MIX_WIDTH, D), MIX_WIDTH ** -0.5),
        'router_w': nrm((L, D, E), D ** -0.5),
        'router_b': nrm((L, E), 0.01),
        'moe_w_gu': nrm((L, E, D, 2 * F), D ** -0.5),
        'moe_b_gu': nrm((L, E, 2 * F), 0.02),
        'moe_w_dn': nrm((L, E, F, D), F ** -0.5),
        'moe_b_dn': nrm((L, E, D), 0.02),
    }


def reference(x_prompt, x_sample, cache_fox, cache_diff, cache_mla, page_table,
              c_prompt, c_sample, ada_w, ada_b, norm1_g, norm2_g, w_in, fox_fb,
              fox_qn, fox_kn, diff_qn, diff_kn, diff_lam, diff_subln,
              mla_cqn, mla_ckvn, mla_krn, mla_qn, mla_wuq, mla_wuk, mla_wuv,
              w_out, router_w, router_b, moe_w_gu, moe_b_gu, moe_w_dn, moe_b_dn):
    pos_prompt = jnp.arange(x_prompt.shape[1])
    pos_sample = PAST_LEN + jnp.arange(x_sample.shape[1])
    xp, xs = x_prompt, x_sample
    rows_p, rows_s = [], []
    for l in range(DEPTH):
        p = dict(ada_w=ada_w[l], ada_b=ada_b[l], norm1_g=norm1_g[l], norm2_g=norm2_g[l],
                 w_in=w_in[l], fox_fb=fox_fb[l], fox_qn=fox_qn[l], fox_kn=fox_kn[l],
                 diff_qn=diff_qn[l], diff_kn=diff_kn[l], diff_lam=diff_lam[l],
                 diff_subln=diff_subln[l], mla_cqn=mla_cqn[l], mla_ckvn=mla_ckvn[l],
                 mla_krn=mla_krn[l], mla_qn=mla_qn[l], mla_wuq=mla_wuq[l],
                 mla_wuk=mla_wuk[l], mla_wuv=mla_wuv[l], w_out=w_out[l],
                 router_w=router_w[l], router_b=router_b[l], moe_w_gu=moe_w_gu[l],
                 moe_b_gu=moe_b_gu[l], moe_w_dn=moe_w_dn[l], moe_b_dn=moe_b_dn[l])
        lam_init = 0.8 - 0.6 * math.exp(-0.3 * l)
        xp, rp = trunk_layer(xp, c_prompt, pos_prompt, p, lam_init, attend_prompt)
        past = (gather_pages(cache_fox[l], page_table),
                gather_pages(cache_diff[l], page_table),
                gather_pages(cache_mla[l], page_table))
        xs, rs = trunk_layer(xs, c_sample, pos_sample, p, lam_init,
                             functools.partial(attend_sample, past))
        rows_p.append(rp)
        rows_s.append(rs)
    fox_p = jnp.stack([r[0] for r in rows_p])
    diff_p = jnp.stack([r[1] for r in rows_p])
    mla_p = jnp.stack([r[2] for r in rows_p])
    fox_s = jnp.stack([r[0] for r in rows_s])
    diff_s = jnp.stack([r[1] for r in rows_s])
    mla_s = jnp.stack([r[2] for r in rows_s])
    return (xp, xs, fox_p, diff_p, mla_p, fox_s, diff_s, mla_s)
```

```python
import functools
import math

import jax
import jax.numpy as jnp
from jax import lax
from jax.experimental import pallas as pl
from jax.experimental.pallas import tpu as pltpu

F32 = jnp.float32
BF16 = jnp.bfloat16
I32 = jnp.int32

LANES = 128
SUBLANES = 8
VMEM_BYTES_V7X = 64 * 1024 * 1024
VMEM_LIMIT = 56 * 1024 * 1024

HEAD64 = 64
MLA_NOPE = 128
MLA_QK = MLA_NOPE + HEAD64
TOP_K = 4
SWIGLU_ALPHA = 1.702
SWIGLU_LIMIT = 7.0
ROPE_THETA = 10000.0
NORM_EPS = 1e-6
NEG_BIG = -1e30

TOKEN_TILE = 128
PAGE = 128
ATTN_TILE = 256
MOE_TILE = 256
DECODE_PAGES = 16


def _cparams(sem, vmem=VMEM_LIMIT):
    return pltpu.CompilerParams(dimension_semantics=sem, vmem_limit_bytes=vmem)


def _dot(a, b):
    return jnp.dot(a, b, preferred_element_type=F32)


def _dot_nt(a, b):
    return lax.dot_general(a, b, (((1,), (1,)), ((), ())), preferred_element_type=F32)


def _split2(x):
    hi = x.astype(BF16)
    lo = (x - hi.astype(F32)).astype(BF16)
    return hi, lo


def _split3(x):
    hi = x.astype(BF16)
    r = x - hi.astype(F32)
    mid = r.astype(BF16)
    lo = (r - mid.astype(F32)).astype(BF16)
    return hi, mid, lo


def _dot2(x, m):
    hi, lo = _split2(x)
    return _dot(hi, m) + _dot(lo, m)


def _lane(shape):
    return lax.broadcasted_iota(I32, shape, len(shape) - 1)


def _ada_kernel(c_ref, w_ref, b_ref, o_ref):
    c = c_ref[...]
    a = (c * jax.nn.sigmoid(c)).astype(BF16)
    o_ref[...] = _dot(a, w_ref[...].astype(BF16)) + b_ref[...]


def _ada_mod(c_all, ada_w, ada_b):
    depth, d, n = ada_w.shape
    r = c_all.shape[0]
    tn = 512
    return pl.pallas_call(
        _ada_kernel,
        grid=(depth, n // tn),
        in_specs=[
            pl.BlockSpec((r, d), lambda l, j: (0, 0)),
            pl.BlockSpec((None, d, tn), lambda l, j: (l, 0, j)),
            pl.BlockSpec((None, 1, tn), lambda l, j: (l, 0, j)),
        ],
        out_specs=pl.BlockSpec((None, r, tn), lambda l, j: (l, 0, j)),
        out_shape=jax.ShapeDtypeStruct((depth, r, n), F32),
        compiler_params=_cparams(("arbitrary", "arbitrary")),
        name="ada_mod",
    )(c_all, ada_w, ada_b.reshape(depth, 1, n))


class _ModSpecs:
    def __init__(self, layer, d, n_prompt_tiles, tiles_per_seq, n_seq):
        self.layer, self.d = layer, d
        self.n_prompt_tiles, self.tiles_per_seq, self.n_seq = n_prompt_tiles, tiles_per_seq, n_seq

    def sample(self, piece):
        l = self.layer
        return pl.BlockSpec((None, TOKEN_TILE, self.d), lambda i: (l, 0, piece))

    def prompt(self, piece):
        l, tps, last = self.layer, self.tiles_per_seq, self.n_seq - 1
        return pl.BlockSpec((None, None, 1, self.d),
                            lambda i: (l, TOKEN_TILE + jnp.minimum(i // tps, last), 0, piece))


def _pick(is_sample, sample_ref, prompt_ref):
    return jnp.where(is_sample, sample_ref[...], prompt_ref[...])


def _rms(x):
    return x * lax.rsqrt(jnp.mean(x * x, axis=-1, keepdims=True) + NORM_EPS)


def _prenorm1_kernel(n_prompt_tiles, x_ref, g_ref, shs_ref, shp_ref, scs_ref, scp_ref, h_ref):
    is_sample = pl.program_id(0) >= n_prompt_tiles
    shift = _pick(is_sample, shs_ref, shp_ref)
    scale = _pick(is_sample, scs_ref, scp_ref)
    h_ref[...] = (_rms(x_ref[...]) * g_ref[...] * (1.0 + scale) + shift).astype(h_ref.dtype)


def _prenorm1(x, norm_g, mod, mod4, ms):
    t, d = x.shape
    tile = pl.BlockSpec((TOKEN_TILE, d), lambda i: (i, 0))
    return pl.pallas_call(
        functools.partial(_prenorm1_kernel, ms.n_prompt_tiles),
        grid=(t // TOKEN_TILE,),
        in_specs=[tile, pl.BlockSpec((1, d), lambda i: (0, 0)),
                  ms.sample(0), ms.prompt(0), ms.sample(1), ms.prompt(1)],
        out_specs=tile,
        out_shape=jax.ShapeDtypeStruct((t, d), BF16),
        compiler_params=_cparams(("arbitrary",)),
        name="prenorm1",
    )(x, norm_g, mod, mod4, mod, mod4)


def _mm_kernel(a_ref, w_ref, o_ref, wb_ref):
    @pl.when(pl.program_id(1) == 0)
    def _():
        wb_ref[...] = w_ref[...].astype(BF16)

    o_ref[...] = _dot(a_ref[...], wb_ref[...]).astype(o_ref.dtype)


def _matmul(a, w, layer, tm, tn, out_dtype, name):
    m, k = a.shape
    n = w.shape[-1]
    assert m % tm == 0 and n % tn == 0
    return pl.pallas_call(
        _mm_kernel,
        grid=(n // tn, m // tm),
        in_specs=[pl.BlockSpec((tm, k), lambda j, i: (i, 0)),
                  pl.BlockSpec((None, k, tn), lambda j, i: (layer, 0, j), pipeline_mode=pl.Buffered(1))],
        out_specs=pl.BlockSpec((tm, tn), lambda j, i: (i, j)),
        out_shape=jax.ShapeDtypeStruct((m, n), out_dtype),
        scratch_shapes=[pltpu.VMEM((k, tn), BF16)],
        compiler_params=_cparams(("arbitrary", "arbitrary")),
        name=name,
    )(a, w)


def _rope(x, c, sa, sb):
    return x * c + pltpu.roll(x, 32, 1) * sa + pltpu.roll(x, 96, 1) * sb


def _rms64(x, b64):
    return x * lax.rsqrt(_dot2(x * x, b64) * (1.0 / HEAD64) + NORM_EPS)


def _prep_kernel(cfg, z_ref, c_ref, sa_ref, sb_ref, b64_ref, tri_ref,
                 indn_ref, indnt_ref, indr_ref, indrt_ref,
                 fqn_ref, fkn_ref, dqn_ref, dkn_ref, cqn_ref, ckvn_ref, krn_ref, qnn_ref, qnr_ref, fb_ref,
                 wuq_ref, wuk_ref,
                 qf_ref, qd_ref, qlat_ref, qrope_ref, frow_ref, drow_ref, mrow_ref,
                 fkk_ref, fvv_ref, dk_ref, dv_ref, mk_ref, cum_ref,
                 wuq_bf, wuk_bf, carry_ref):
    (n_prompt_tiles, tiles_per_seq, n_fox, n_diff, n_mla, q_rank, off) = cfg
    i = pl.program_id(0)
    is_sample = i >= n_prompt_tiles
    tm = TOKEN_TILE

    @pl.when(i == 0)
    def _():
        wuq_bf[...] = wuq_ref[...].astype(BF16)
        wuk_bf[...] = wuk_ref[...].astype(BF16)

    c, sa, sb = c_ref[...], sa_ref[...], sb_ref[...]
    b64 = b64_ref[...]
    lane = _lane((tm, LANES))
    left = lane < HEAD64

    for s in range(n_fox // 2):
        x = z_ref[:, off["fq"] + s * LANES: off["fq"] + (s + 1) * LANES]
        qf_ref[:, s * LANES:(s + 1) * LANES] = (_rms64(x, b64) * (fqn_ref[...] * 0.125)).astype(BF16)

    for s in range(n_diff):
        x = z_ref[:, off["dq"] + s * LANES: off["dq"] + (s + 1) * LANES]
        y = _rope(_rms64(x, b64) * dqn_ref[...], c, sa, sb)
        qd_ref[:, s * LANES:(s + 1) * LANES] = (y * 0.125).astype(BF16)

    kv = z_ref[:, off["fkv"]: off["fkv"] + LANES]
    kvn = jnp.where(left, _rms64(kv, b64) * fkn_ref[...], kv)
    kv_sw = pltpu.roll(kvn, HEAD64, 1)
    fkk_ref[...] = jnp.where(left, kvn, kv_sw).astype(BF16)
    fvv_ref[...] = jnp.where(left, kv_sw, kvn).astype(BF16)
    tail = z_ref[:, off["tail"]: off["tail"] + LANES]
    ff = tail[:, HEAD64:HEAD64 + n_fox] + fb_ref[...]
    logf = jnp.minimum(ff, 0.0) - jnp.log(1.0 + jnp.exp(-jnp.abs(ff)))
    frow_ref[:, 0:LANES] = kvn
    frow_ref[:, LANES:LANES + n_fox] = logf

    @pl.when(i % tiles_per_seq == 0)
    def _():
        carry_ref[...] = jnp.zeros_like(carry_ref)

    hi, mid, lo = _split3(logf)
    tri = tri_ref[...]
    cum = carry_ref[...] + _dot(tri, hi) + _dot(tri, mid) + _dot(tri, lo)
    carry_ref[...] = cum[tm - 1:tm, :]
    cum_ref[...] = jnp.where(is_sample, logf, cum)

    dk = _rope(_rms64(z_ref[:, off["dk"]: off["dk"] + LANES], b64) * dkn_ref[...], c, sa, sb)
    dv = z_ref[:, off["dv"]: off["dv"] + LANES]
    drow_ref[:, 0:LANES] = dk
    drow_ref[:, LANES:2 * LANES] = dv
    dk_ref[...] = dk.astype(BF16)
    dv_ref[...] = dv.astype(BF16)

    ckv = _rms(z_ref[:, off["ckv"]: off["ckv"] + LANES]) * ckvn_ref[...]
    kr = _rope(_rms64(tail, b64) * krn_ref[...], c, sa, sb)
    mrow_ref[:, 0:LANES] = ckv
    mrow_ref[:, LANES:LANES + HEAD64] = kr[:, 0:HEAD64]
    mk_ref[:, 0:LANES] = ckv.astype(BF16)
    mk_ref[:, LANES:2 * LANES] = jnp.where(left, kr, pltpu.roll(kr, HEAD64, 1)).astype(BF16)

    cq = (_rms(z_ref[:, off["cq"]: off["cq"] + q_rank]) * cqn_ref[...]).astype(BF16)
    q = _dot(cq, wuq_bf[...])
    nn = n_mla * MLA_NOPE
    q_n, q_r = q[:, :nn], q[:, nn:]
    ss = _dot2(q_n * q_n, indn_ref[...]) + _dot2(q_r * q_r, indr_ref[...])
    r = lax.rsqrt(ss * (1.0 / MLA_QK) + NORM_EPS)
    mla_scale = MLA_QK ** -0.5
    r_n = _dot2(r, indnt_ref[...])
    for h in range(n_mla):
        qh = (q_n[:, h * LANES:(h + 1) * LANES] * r_n[:, h * LANES:(h + 1) * LANES] * qnn_ref[...]).astype(BF16)
        qlat_ref[:, h * LANES:(h + 1) * LANES] = (_dot(qh, wuk_bf[h]) * mla_scale).astype(BF16)
    r_r = _dot2(r, indrt_ref[...])
    for s in range(n_mla // 2):
        sl = slice(s * LANES, (s + 1) * LANES)
        y = _rope(q_r[:, sl] * r_r[:, sl] * qnr_ref[...], c, sa, sb)
        qrope_ref[:, sl] = (y * mla_scale).astype(BF16)


def _prep(z, tabs, consts, gains, wuq_p, wuk_t, cfg, dims):
    t, zc = z.shape
    (n_prompt_tiles, tiles_per_seq, n_fox, n_diff, n_mla, q_rank, off) = cfg
    tm = TOKEN_TILE
    row = lambda w: pl.BlockSpec((tm, w), lambda i: (i, 0))
    full = lambda a: pl.BlockSpec(a.shape, lambda i: (0,) * a.ndim)
    in_arrays = [z, *tabs, *consts, *gains, wuq_p, wuk_t]
    in_specs = [row(zc), row(LANES), row(LANES), row(LANES)] + [full(a) for a in in_arrays[4:]]
    widths = [("qf", n_fox * HEAD64, BF16), ("qd", n_diff * LANES, BF16), ("qlat", n_mla * LANES, BF16),
              ("qrope", n_mla * HEAD64, BF16), ("frow", dims["fox_row"], F32), ("drow", dims["diff_row"], F32),
              ("mrow", dims["mla_row"], F32), ("fkk", LANES, BF16), ("fvv", LANES, BF16), ("dk", LANES, BF16),
              ("dv", LANES, BF16), ("mk", 2 * LANES, BF16), ("cum", n_fox, F32)]
    outs = pl.pallas_call(
        functools.partial(_prep_kernel, cfg),
        grid=(t // tm,),
        in_specs=in_specs,
        out_specs=[row(w) for _, w, _ in widths],
        out_shape=[jax.ShapeDtypeStruct((t, w), dt) for _, w, dt in widths],
        scratch_shapes=[pltpu.VMEM(wuq_p.shape, BF16), pltpu.VMEM(wuk_t.shape, BF16), pltpu.VMEM((1, n_fox), F32)],
        compiler_params=_cparams(("arbitrary",)),
        name="qkv_prep",
    )(*in_arrays)
    return {k: v for (k, _, _), v in zip(widths, outs)}


def _online_update(s, m, l):
    m_new = jnp.maximum(m, jnp.max(s, axis=-1, keepdims=True))
    p = jnp.exp(s - m_new)
    alpha = jnp.exp(m - m_new)
    return m_new, alpha * l + jnp.sum(p, axis=-1, keepdims=True), alpha, p


def _causal_mask(tq):
    r = lax.broadcasted_iota(I32, (tq, tq), 0)
    c = lax.broadcasted_iota(I32, (tq, tq), 1)
    return c <= r


def _fox_attn_kernel(n_heads, q_ref, kk_ref, vv_ref, cq_ref, ckt_ref, o_ref):
    tq = ATTN_TILE
    i = pl.program_id(1)
    lane = _lane((tq, LANES))
    left = lane < HEAD64
    mask = _causal_mask(tq)
    for s in range(n_heads // 2):
        qs = q_ref[:, s * LANES:(s + 1) * LANES]
        outs = []
        for half in range(2):
            h = 2 * s + half
            qh = jnp.where(left if half == 0 else jnp.logical_not(left), qs, jnp.zeros_like(qs))
            cq = cq_ref[:, h:h + 1]

            def scores(j, qh=qh, cq=cq, h=h):
                k = kk_ref[pl.ds(pl.multiple_of(j * tq, tq), tq), :]
                return _dot_nt(qh, k) + cq - ckt_ref[j, h:h + 1, :]

            def accumulate(j, sc, carry):
                m, l, acc = carry
                m, l, alpha, p = _online_update(sc, m, l)
                v = vv_ref[pl.ds(pl.multiple_of(j * tq, tq), tq), :]
                return m, l, alpha * acc + _dot(p.astype(BF16), v)

            def body(j, carry):
                return accumulate(j, scores(j), carry)

            init = (jnp.full((tq, 1), NEG_BIG, F32), jnp.zeros((tq, 1), F32), jnp.zeros((tq, LANES), F32))
            carry = lax.fori_loop(0, i, body, init)
            m, l, acc = accumulate(i, jnp.where(mask, scores(i), NEG_BIG), carry)
            outs.append(acc / l)
        o_ref[:, s * LANES:(s + 1) * LANES] = jnp.where(left, outs[0], outs[1]).astype(o_ref.dtype)


def _diff_lambda(lam_ref, lam_init):
    lv = lam_ref[...]
    a = jnp.sum(jnp.sum(lv[0:1] * lv[1:2], axis=1, keepdims=True), axis=0, keepdims=True)
    b = jnp.sum(jnp.sum(lv[2:3] * lv[3:4], axis=1, keepdims=True), axis=0, keepdims=True)
    return jnp.exp(a) - jnp.exp(b) + lam_init


def _diff_attn_kernel(n_heads, lam_init, q_ref, k_ref, v_ref, lam_ref, sub_ref, o_ref):
    tq = ATTN_TILE
    i = pl.program_id(1)
    lane = _lane((tq, LANES))
    left = lane < HEAD64
    mask = _causal_mask(tq)
    lam = _diff_lambda(lam_ref, lam_init)
    for h in range(n_heads):
        qs = q_ref[:, h * LANES:(h + 1) * LANES]
        q1 = jnp.where(left, qs, jnp.zeros_like(qs))
        q2 = jnp.where(left, jnp.zeros_like(qs), qs)

        def step(j, carry, masked, q1=q1, q2=q2):
            rows = pl.ds(pl.multiple_of(j * tq, tq), tq)
            k = k_ref[rows, :]
            v = v_ref[rows, :]
            new = []
            for q, (m, l, acc) in zip((q1, q2), carry):
                sc = _dot_nt(q, k)
                if masked:
                    sc = jnp.where(mask, sc, NEG_BIG)
                m, l, alpha, p = _online_update(sc, m, l)
                new.append((m, l, alpha * acc + _dot(p.astype(BF16), v)))
            return tuple(new)

        one = (jnp.full((tq, 1), NEG_BIG, F32), jnp.zeros((tq, 1), F32), jnp.zeros((tq, LANES), F32))
        carry = lax.fori_loop(0, i, lambda j, c: step(j, c, False), (one, one))
        (m1, l1, a1), (m2, l2, a2) = step(i, carry, True)
        od = a1 / l1 - lam * (a2 / l2)
        o_ref[:, h * LANES:(h + 1) * LANES] = (_rms(od) * sub_ref[...] * (1.0 - lam_init)).astype(o_ref.dtype)


def _mla_attn_kernel(n_heads, qlat_ref, qrope_ref, k_ref, wuv_ref, o_ref):
    tq = ATTN_TILE
    i = pl.program_id(1)
    lane = _lane((tq, LANES))
    left = lane < HEAD64
    mask = _causal_mask(tq)
    for h in range(n_heads):
        qr = qrope_ref[:, (h // 2) * LANES:(h // 2 + 1) * LANES]
        qr = jnp.where(left if h % 2 == 0 else jnp.logical_not(left), qr, jnp.zeros_like(qr))
        q = jnp.concatenate([qlat_ref[:, h * LANES:(h + 1) * LANES], qr], axis=-1)

        def step(j, carry, masked, q=q):
            m, l, acc = carry
            k = k_ref[pl.ds(pl.multiple_of(j * tq, tq), tq), :]
            sc = _dot_nt(q, k)
            if masked:
                sc = jnp.where(mask, sc, NEG_BIG)
            m, l, alpha, p = _online_update(sc, m, l)
            return m, l, alpha * acc + _dot(p.astype(BF16), k[:, 0:LANES])

        init = (jnp.full((tq, 1), NEG_BIG, F32), jnp.zeros((tq, 1), F32), jnp.zeros((tq, LANES), F32))
        carry = lax.fori_loop(0, i, lambda j, c: step(j, c, False), init)
        m, l, acc = step(i, carry, True)
        ol = (acc / l).astype(BF16)
        o_ref[:, h * LANES:(h + 1) * LANES] = _dot(ol, wuv_ref[h]).astype(o_ref.dtype)


def _prompt_attention(pp, cum_t, diff_lam, diff_subln, wuv_h, lam_init, n_seq, seq, heads):
    n_fox, n_diff, n_mla = heads
    tq = ATTN_TILE
    nq = seq // tq
    tp = n_seq * seq
    qrow = lambda w: pl.BlockSpec((tq, w), lambda b, i: (b * nq + i, 0))
    seqrows = lambda w: pl.BlockSpec((seq, w), lambda b, i: (b, 0))
    full = lambda a: pl.BlockSpec(a.shape, lambda b, i: (0,) * a.ndim)
    cp = _cparams(("arbitrary", "arbitrary"))
    of = pl.pallas_call(
        functools.partial(_fox_attn_kernel, n_fox),
        grid=(n_seq, nq),
        in_specs=[qrow(n_fox * HEAD64), seqrows(LANES), seqrows(LANES), qrow(n_fox),
                  pl.BlockSpec((None, nq, n_fox, tq), lambda b, i: (b, 0, 0, 0))],
        out_specs=qrow(n_fox * HEAD64),
        out_shape=jax.ShapeDtypeStruct((tp, n_fox * HEAD64), BF16),
        compiler_params=cp, name="fox_attn",
    )(pp["qf"], pp["fkk"], pp["fvv"], pp["cum"], cum_t)
    od = pl.pallas_call(
        functools.partial(_diff_attn_kernel, n_diff, lam_init),
        grid=(n_seq, nq),
        in_specs=[qrow(n_diff * LANES), seqrows(LANES), seqrows(LANES), full(diff_lam), full(diff_subln)],
        out_specs=qrow(n_diff * LANES),
        out_shape=jax.ShapeDtypeStruct((tp, n_diff * LANES), BF16),
        compiler_params=cp, name="diff_attn",
    )(pp["qd"], pp["dk"], pp["dv"], diff_lam, diff_subln)
    om = pl.pallas_call(
        functools.partial(_mla_attn_kernel, n_mla),
        grid=(n_seq, nq),
        in_specs=[qrow(n_mla * LANES), qrow(n_mla * HEAD64), seqrows(2 * LANES), full(wuv_h)],
        out_specs=qrow(n_mla * LANES),
        out_shape=jax.ShapeDtypeStruct((tp, n_mla * LANES), BF16),
        compiler_params=cp, name="mla_attn",
    )(pp["qlat"], pp["qrope"], pp["mk"], wuv_h)
    return of, od, om


def _page_copy(cache_ref, buf_ref, sem_ref, layer, page, slot, g):
    return pltpu.make_async_copy(cache_ref.at[layer, page], buf_ref.at[slot, g], sem_ref.at[slot])


def _decode_kernel(cfg, pt_ref, qf_ref, qd_ref, qlat_ref, qrope_ref, fnew_ref, dnew_ref, mnew_ref, lfnew_ref, uext_ref,
                   cf_ref, cd_ref, cm_ref,
                   of_ref, od_ref, ol_ref,
                   fbuf, dbuf, mbuf, sem_f, sem_d, sem_m, st_ref, carry_ref):
    (layer, n_pages, n_chunks, n_heads) = cfg
    gpp = DECODE_PAGES
    t = pl.program_id(0)
    n_steps = pl.num_programs(0)
    b = t // n_chunks
    c = t % n_chunks
    slot = t % 2

    def issue(step, sl, start):
        sb = step // n_chunks
        sc = step % n_chunks
        base = sb * n_pages + (n_chunks - 1 - sc) * gpp
        for g in range(gpp):
            page = pt_ref[base + g]
            for cache, buf, sem in ((cf_ref, fbuf, sem_f), (cd_ref, dbuf, sem_d), (cm_ref, mbuf, sem_m)):
                cp = _page_copy(cache, buf, sem, layer, page, sl, g)
                if start:
                    cp.start()
                else:
                    cp.wait()

    @pl.when(t == 0)
    def _():
        issue(t, slot, True)

    @pl.when(t + 1 < n_steps)
    def _():
        issue(t + 1, 1 - slot, True)

    qf = qf_ref[...]
    qd = qd_ref[...]
    qlat = qlat_ref[...]
    qrope = qrope_ref[...]
    hh = n_heads

    @pl.when(c == 0)
    def _():
        fnew = fnew_ref[...]
        kvn = fnew[:, 0:LANES].astype(BF16).astype(F32)
        dnew = dnew_ref[...]
        mnew = mnew_ref[...]
        s_f = jnp.sum(qf.astype(F32) * kvn, axis=-1, keepdims=True)
        s_d = jnp.sum(qd.astype(F32) * dnew[:, 0:LANES].astype(BF16).astype(F32), axis=-1, keepdims=True)
        latn = mnew[:, 0:LANES].astype(BF16).astype(F32)
        s_m = (jnp.sum(qlat.astype(F32) * latn, axis=-1, keepdims=True)
               + jnp.sum(qrope.astype(F32) * mnew[:, LANES:LANES + HEAD64].astype(BF16).astype(F32),
                         axis=-1, keepdims=True))
        ones = jnp.ones((hh, LANES), F32)
        for k, (sv, vrow) in enumerate(((s_f, kvn), (s_d, dnew[:, LANES:2 * LANES].astype(BF16).astype(F32)),
                                        (s_m, latn))):
            st_ref[3 * k + 0] = sv * ones
            st_ref[3 * k + 1] = ones
            st_ref[3 * k + 2] = vrow * ones
        carry_ref[...] = jnp.zeros_like(carry_ref)

    issue(t, slot, False)

    lfnew = lfnew_ref[...]
    uext = uext_ref[...]
    lane_pad = jnp.zeros((PAGE, LANES - hh), F32)

    def update(k, sc, v):
        m, l, acc = st_ref[3 * k + 0], st_ref[3 * k + 1], st_ref[3 * k + 2]
        m_new = jnp.maximum(m, jnp.max(sc, axis=-1, keepdims=True))
        p = jnp.exp(sc - m_new)
        alpha = jnp.exp(m - m_new)
        st_ref[3 * k + 0] = m_new
        st_ref[3 * k + 1] = alpha * l + jnp.sum(p, axis=-1, keepdims=True)
        st_ref[3 * k + 2] = alpha * acc + _dot(p.astype(BF16), v)

    for g in reversed(range(gpp)):
        fp = fbuf[slot, g]
        kv = fp[:, 0:LANES].astype(BF16)
        lft = jnp.transpose(jnp.concatenate([fp[:, LANES:LANES + hh], lane_pad], axis=1))[0:hh, :]
        hi, mid, lo = _split3(lft)
        sfx = _dot(jnp.concatenate([hi, mid, lo], axis=0), uext)
        sfx = sfx[0:hh] + sfx[hh:2 * hh] + sfx[2 * hh:3 * hh]
        carry = carry_ref[...]
        decay = lfnew + carry + sfx[:, 0:LANES]
        carry_ref[...] = carry + sfx[:, LANES:2 * LANES]
        update(0, _dot_nt(qf, kv) + decay, kv)

        dp = dbuf[slot, g]
        update(1, _dot_nt(qd, dp[:, 0:LANES].astype(BF16)), dp[:, LANES:2 * LANES].astype(BF16))

        mp = mbuf[slot, g]
        lat = mp[:, 0:LANES].astype(BF16)
        sc = _dot_nt(qlat, lat) + _dot_nt(qrope, mp[:, LANES:LANES + HEAD64].astype(BF16))
        update(2, sc, lat)

    @pl.when(c == n_chunks - 1)
    def _():
        of_ref[...] = st_ref[2] / st_ref[1]
        od_ref[...] = st_ref[5] / st_ref[4]
        ol_ref[...] = st_ref[8] / st_ref[7]


def _decode_attention(layer, page_table, qf, qd, qlat, qrope, fnew, dnew, mnew, lfnew, uext, cache_fox, cache_diff, cache_mla):
    nb, n_pages = page_table.shape
    hh = qf.shape[1]
    gpp = DECODE_PAGES
    n_chunks = n_pages // gpp
    fw, dw, mw = cache_fox.shape[-1], cache_diff.shape[-1], cache_mla.shape[-1]
    cfg = (layer, n_pages, n_chunks, hh)
    per_b = lambda a: pl.BlockSpec((None,) + a.shape[1:], lambda t, pt: (t // n_chunks,) + (0,) * (a.ndim - 1))
    anyspec = pl.BlockSpec(memory_space=pl.ANY)
    out = jax.ShapeDtypeStruct((nb, hh, LANES), F32)
    grid_spec = pltpu.PrefetchScalarGridSpec(
        num_scalar_prefetch=1,
        grid=(nb * n_chunks,),
        in_specs=[per_b(qf), per_b(qd), per_b(qlat), per_b(qrope), per_b(fnew), per_b(dnew), per_b(mnew), per_b(lfnew),
                  pl.BlockSpec(uext.shape, lambda t, pt: (0, 0)), anyspec, anyspec, anyspec],
        out_specs=[pl.BlockSpec((None, hh, LANES), lambda t, pt: (t // n_chunks, 0, 0))] * 3,
        scratch_shapes=[pltpu.VMEM((2, gpp, PAGE, fw), F32), pltpu.VMEM((2, gpp, PAGE, dw), F32),
                        pltpu.VMEM((2, gpp, PAGE, mw), F32),
                        pltpu.SemaphoreType.DMA((2,)), pltpu.SemaphoreType.DMA((2,)), pltpu.SemaphoreType.DMA((2,)),
                        pltpu.VMEM((9, hh, LANES), F32), pltpu.VMEM((hh, LANES), F32)],
    )
    return pl.pallas_call(
        functools.partial(_decode_kernel, cfg),
        grid_spec=grid_spec,
        out_shape=[out, out, out],
        compiler_params=_cparams(("arbitrary",)),
        name="decode_attn",
    )(page_table.reshape(-1), qf, qd, qlat, qrope, fnew, dnew, mnew, lfnew, uext, cache_fox, cache_diff, cache_mla)


def _sample_merge_kernel(n_diff, n_mla, lam_init, of_ref, o1_ref, o2_ref, ol_ref, lam_ref, sub_ref, wuv_ref, o_ref):
    lam = _diff_lambda(lam_ref, lam_init)
    nf = of_ref.shape[1]
    o_ref[:, 0:nf] = of_ref[...].astype(o_ref.dtype)
    for h in range(n_diff):
        sl = slice(h * LANES, (h + 1) * LANES)
        od = o1_ref[:, sl] - lam * o2_ref[:, sl]
        o_ref[:, nf + h * LANES: nf + (h + 1) * LANES] = (_rms(od) * sub_ref[...] * (1.0 - lam_init)).astype(o_ref.dtype)
    base = nf + n_diff * LANES
    for h in range(n_mla):
        sl = slice(h * LANES, (h + 1) * LANES)
        o_ref[:, base + h * LANES: base + (h + 1) * LANES] = _dot(ol_ref[:, sl].astype(BF16), wuv_ref[h]).astype(o_ref.dtype)


def _sample_merge(of, o1, o2, ol, diff_lam, diff_subln, wuv_h, lam_init, d):
    nb = of.shape[0]
    n_diff = o1.shape[1] // LANES
    n_mla = ol.shape[1] // LANES
    args = (of, o1, o2, ol, diff_lam, diff_subln, wuv_h)
    return pl.pallas_call(
        functools.partial(_sample_merge_kernel, n_diff, n_mla, lam_init),
        grid=(1,),
        in_specs=[pl.BlockSpec(a.shape, lambda i, nd=a.ndim: (0,) * nd) for a in args],
        out_specs=pl.BlockSpec((nb, d), lambda i: (0, 0)),
        out_shape=jax.ShapeDtypeStruct((nb, d), BF16),
        compiler_params=_cparams(("arbitrary",)),
        name="sample_merge",
    )(*args)


def _prenorm2_kernel(n_prompt_tiles, n_exp, x_ref, ao_ref, g_ref, g1s_ref, g1p_ref, shs_ref, shp_ref, scs_ref, scp_ref,
                     rw_hi_ref, rw_lo_ref, rb_ref, tri_ref,
                     x1_ref, h_ref, ti_ref, gate_ref, rank_ref, cnt_ref, base_ref):
    i = pl.program_id(0)
    is_sample = i >= n_prompt_tiles
    tm = TOKEN_TILE

    @pl.when(i == 0)
    def _():
        base_ref[...] = jnp.zeros_like(base_ref)

    x1 = x_ref[...] + _pick(is_sample, g1s_ref, g1p_ref) * ao_ref[...]
    x1_ref[...] = x1
    h = _rms(x1) * g_ref[...] * (1.0 + _pick(is_sample, scs_ref, scp_ref)) + _pick(is_sample, shs_ref, shp_ref)
    h_ref[...] = h

    h_hi, h_lo = _split2(h)
    logits = _dot(h_hi, rw_hi_ref[...]) + _dot(h_lo, rw_hi_ref[...]) + _dot(h_hi, rw_lo_ref[...]) + rb_ref[...]
    lane = _lane((tm, n_exp))
    lane_k = _lane((tm, TOP_K))
    work = logits
    vals, sels = [], []
    top_i = jnp.zeros((tm, TOP_K), I32)
    for k in range(TOP_K):
        v = jnp.max(work, axis=-1, keepdims=True)
        idx = jnp.min(jnp.where(work == v, lane, n_exp), axis=-1, keepdims=True)
        sel = lane == idx
        vals.append(v)
        sels.append(sel)
        top_i = jnp.where(lane_k == k, idx, top_i)
        work = jnp.where(sel, -jnp.inf, work)
    es = [jnp.exp(v - vals[0]) for v in vals]
    tot = es[0] + es[1] + es[2] + es[3]
    gates = jnp.zeros((tm, TOP_K), F32)
    for k in range(TOP_K):
        gates = jnp.where(lane_k == k, es[k] / tot, gates)

    onehot = jnp.zeros((tm, n_exp), F32)
    for sel in sels:
        onehot = onehot + sel.astype(F32)
    before = _dot(tri_ref[...], onehot.astype(BF16)) + base_ref[...]
    rank = jnp.zeros((tm, TOP_K), F32)
    for k in range(TOP_K):
        rk = jnp.sum(jnp.where(sels[k], before, 0.0), axis=-1, keepdims=True)
        rank = jnp.where(lane_k == k, rk, rank)
    base_ref[...] = base_ref[...] + jnp.sum(onehot, axis=0, keepdims=True)
    ti_ref[...] = top_i
    gate_ref[...] = gates
    rank_ref[...] = rank.astype(I32)
    cnt_ref[...] = base_ref[...]


def _prenorm2(x, ao, norm_g, mod, mod4, ms, rw_hi, rw_lo, rb, tri_strict):
    t, d = x.shape
    n_exp = rw_hi.shape[1]
    tile = pl.BlockSpec((TOKEN_TILE, d), lambda i: (i, 0))
    small = pl.BlockSpec((TOKEN_TILE, TOP_K), lambda i: (i, 0))
    full = lambda a: pl.BlockSpec(a.shape, lambda i: (0,) * a.ndim)
    return pl.pallas_call(
        functools.partial(_prenorm2_kernel, ms.n_prompt_tiles, n_exp),
        grid=(t // TOKEN_TILE,),
        in_specs=[tile, tile, full(norm_g), ms.sample(2), ms.prompt(2), ms.sample(3), ms.prompt(3),
                  ms.sample(4), ms.prompt(4), full(rw_hi), full(rw_lo), full(rb), full(tri_strict)],
        out_specs=[tile, tile, small, small, small, pl.BlockSpec((1, n_exp), lambda i: (0, 0))],
        out_shape=[jax.ShapeDtypeStruct((t, d), F32), jax.ShapeDtypeStruct((t, d), F32),
                   jax.ShapeDtypeStruct((t, TOP_K), I32), jax.ShapeDtypeStruct((t, TOP_K), F32),
                   jax.ShapeDtypeStruct((t, TOP_K), I32), jax.ShapeDtypeStruct((1, n_exp), F32)],
        scratch_shapes=[pltpu.VMEM((1, n_exp), F32)],
        compiler_params=_cparams(("arbitrary",)),
        name="prenorm2_router",
    )(x, ao, norm_g, mod, mod4, mod, mod4, mod, mod4, rw_hi, rw_lo, rb, tri_strict)


def _moe_kernel(cfg, te_ref, nt_ref, src_ref, h_ref, wgu_ref, wdn_ref, bgu_ref, bdn_ref, y_ref,
                xbuf, wgu_bf, wdn_bf, stage_gu, stage_dn, sem_x, sem_w):
    (layer, d, f) = cfg
    tm = MOE_TILE
    g = pl.program_id(0)
    e = te_ref[g]
    active = g < nt_ref[0]
    prev = te_ref[jnp.maximum(g - 1, 0)]
    changed = jnp.logical_or(g == 0, e != prev)

    def row_copy(r):
        return pltpu.make_async_copy(h_ref.at[pl.ds(src_ref[g * tm + r], 1), :], xbuf.at[pl.ds(r, 1), :], sem_x)

    @pl.when(active)
    def _():
        def start(r, _):
            row_copy(r).start()
            return 0
        lax.fori_loop(0, tm, start, 0)

    gu_rows, dn_rows = stage_gu.shape[1], stage_dn.shape[1]
    chunks = [(wgu_ref, stage_gu, wgu_bf, k * gu_rows, gu_rows) for k in range(d // gu_rows)]
    chunks += [(wdn_ref, stage_dn, wdn_bf, k * dn_rows, dn_rows) for k in range(f // dn_rows)]

    @pl.when(jnp.logical_and(active, changed))
    def _():
        copies = [pltpu.make_async_copy(w.at[layer, e, pl.ds(r0, nr), :], st.at[k % 2], sem_w.at[k % 2])
                  for k, (w, st, _, r0, nr) in enumerate(chunks)]
        copies[0].start()
        for k, (_, st, dst, r0, nr) in enumerate(chunks):
            if k + 1 < len(chunks):
                copies[k + 1].start()
            copies[k].wait()
            dst[r0:r0 + nr, :] = st[k % 2].astype(BF16)

    @pl.when(active)
    def _():
        def wait(r, _):
            row_copy(r).wait()
            return 0
        lax.fori_loop(0, tm, wait, 0)
        x = xbuf[...].astype(BF16)
        gu = _dot(x, wgu_bf[...]) + bgu_ref[...]
        gg = jnp.minimum(gu[:, :f], SWIGLU_LIMIT)
        uu = jnp.clip(gu[:, f:], -SWIGLU_LIMIT, SWIGLU_LIMIT)
        act = gg * jax.nn.sigmoid(SWIGLU_ALPHA * gg) * (uu + 1.0)
        y_ref[...] = _dot(act.astype(BF16), wdn_bf[...]) + bdn_ref[...]

    @pl.when(jnp.logical_not(active))
    def _():
        y_ref[...] = jnp.zeros_like(y_ref)


def _moe_grouped(layer, tile_expert, n_tiles, src, h2, w_gu, b_gu, w_dn, b_dn):
    depth, n_exp, d, f2 = w_gu.shape
    f = f2 // 2
    g_max = tile_expert.shape[0]
    tm = MOE_TILE
    gu_rows, dn_rows = 1024, 128
    anyspec = pl.BlockSpec(memory_space=pl.ANY)
    grid_spec = pltpu.PrefetchScalarGridSpec(
        num_scalar_prefetch=3,
        grid=(g_max,),
        in_specs=[anyspec, anyspec, anyspec,
                  pl.BlockSpec((None, None, 1, f2), lambda g, te, nt, sr: (layer, te[g], 0, 0)),
                  pl.BlockSpec((None, None, 1, d), lambda g, te, nt, sr: (layer, te[g], 0, 0))],
        out_specs=pl.BlockSpec((tm, d), lambda g, te, nt, sr: (g, 0)),
        scratch_shapes=[pltpu.VMEM((tm, d), F32), pltpu.VMEM((d, f2), BF16), pltpu.VMEM((f, d), BF16),
                        pltpu.VMEM((2, gu_rows, f2), F32), pltpu.VMEM((2, dn_rows, d), F32),
                        pltpu.SemaphoreType.DMA(()), pltpu.SemaphoreType.DMA((2,))],
    )
    return pl.pallas_call(
        functools.partial(_moe_kernel, (layer, d, f)),
        grid_spec=grid_spec,
        out_shape=jax.ShapeDtypeStruct((g_max * tm, d), F32),
        compiler_params=_cparams(("arbitrary",)),
        name="moe_grouped",
    )(tile_expert, n_tiles, src, h2, w_gu, w_dn, b_gu.reshape(depth, n_exp, 1, f2), b_dn.reshape(depth, n_exp, 1, d))


def _combine_kernel(n_prompt_tiles, pos_ref, x1_ref, gate_ref, g2s_ref, g2p_ref, y_ref, o_ref, ybuf, sem):
    i = pl.program_id(0)
    tm = TOKEN_TILE
    is_sample = i >= n_prompt_tiles

    def row_copy(n):
        r, k = n // TOP_K, n % TOP_K
        return pltpu.make_async_copy(y_ref.at[pl.ds(pos_ref[i * tm * TOP_K + n], 1), :],
                                     ybuf.at[k, pl.ds(r, 1), :], sem)

    def start(n, _):
        row_copy(n).start()
        return 0

    def wait(n, _):
        row_copy(n).wait()
        return 0

    lax.fori_loop(0, tm * TOP_K, start, 0)
    lax.fori_loop(0, tm * TOP_K, wait, 0)
    gate = gate_ref[...]
    moe = gate[:, 0:1] * ybuf[0]
    for k in range(1, TOP_K):
        moe = moe + gate[:, k:k + 1] * ybuf[k]
    o_ref[...] = x1_ref[...] + _pick(is_sample, g2s_ref, g2p_ref) * moe


def _combine(pos, x1, gates, y, mod, mod4, ms):
    t, d = x1.shape
    tile = pl.BlockSpec((TOKEN_TILE, d), lambda i, p: (i, 0))
    l, tps, last = ms.layer, ms.tiles_per_seq, ms.n_seq - 1
    grid_spec = pltpu.PrefetchScalarGridSpec(
        num_scalar_prefetch=1,
        grid=(t // TOKEN_TILE,),
        in_specs=[tile, pl.BlockSpec((TOKEN_TILE, TOP_K), lambda i, p: (i, 0)),
                  pl.BlockSpec((None, TOKEN_TILE, d), lambda i, p: (l, 0, 5)),
                  pl.BlockSpec((None, None, 1, d), lambda i, p: (l, TOKEN_TILE + jnp.minimum(i // tps, last), 0, 5)),
                  pl.BlockSpec(memory_space=pl.ANY)],
        out_specs=tile,
        scratch_shapes=[pltpu.VMEM((TOP_K, TOKEN_TILE, d), F32), pltpu.SemaphoreType.DMA(())],
    )
    return pl.pallas_call(
        functools.partial(_combine_kernel, ms.n_prompt_tiles),
        grid_spec=grid_spec,
        out_shape=jax.ShapeDtypeStruct((t, d), F32),
        compiler_params=_cparams(("arbitrary",)),
        name="moe_combine",
    )(pos.reshape(-1), x1, gates, mod, mod4, y)


def _permute_w_in(w_in, sizes):
    (fq, fk, fv, ff, dq, dk, dv, cq, ckv, kr) = sizes
    o = [0]
    for s in sizes:
        o.append(o[-1] + s)
    seg = lambda k: w_in[..., o[k]:o[k + 1]]
    pad = LANES - kr - ff
    parts = [seg(0), seg(4), seg(7), seg(1), seg(2), seg(5), seg(6), seg(8), seg(9), seg(3),
             jnp.zeros(w_in.shape[:-1] + (pad,), w_in.dtype)]
    off, acc = {}, 0
    for name, p in zip(("fq", "dq", "cq", "fkv", "_fv", "dk", "dv", "ckv", "tail", "_ff", "_pad"), parts):
        off[name] = acc
        acc += p.shape[-1]
    return jnp.concatenate(parts, axis=-1), off


def _rope_tables(positions):
    half = HEAD64 // 2
    inv_freq = ROPE_THETA ** (-jnp.arange(half, dtype=F32) / half)
    ang = positions.astype(F32)[:, None] * inv_freq[None, :]
    cos, sin = jnp.cos(ang), jnp.sin(ang)
    zero = jnp.zeros_like(sin)
    c = jnp.concatenate([cos] * 4, axis=-1)
    sa = jnp.concatenate([zero, sin, zero, sin], axis=-1)
    sb = jnp.concatenate([-sin, zero, -sin, zero], axis=-1)
    return c, sa, sb


def _indicator(n_heads, width):
    rows = jnp.arange(n_heads * width) // width
    return (rows[:, None] == jnp.arange(n_heads)[None, :]).astype(BF16)


def kernel(x_prompt, x_sample, cache_fox, cache_diff, cache_mla, page_table, c_prompt, c_sample, ada_w, ada_b,
           norm1_g, norm2_g, w_in, fox_fb, fox_qn, fox_kn, diff_qn, diff_kn, diff_lam, diff_subln, mla_cqn,
           mla_ckvn, mla_krn, mla_qn, mla_wuq, mla_wuk, mla_wuv, w_out, router_w, router_b, moe_w_gu, moe_b_gu,
           moe_w_dn, moe_b_dn):
    n_seq, seq, d = x_prompt.shape
    nb, dec_seq, _ = x_sample.shape
    depth = ada_w.shape[0]
    n_fox = fox_fb.shape[1]
    n_mla = mla_wuq.shape[2]
    q_rank = mla_wuq.shape[1]
    kv_rank = mla_wuk.shape[1]
    n_diff = (w_out.shape[1] - n_fox * HEAD64 - n_mla * LANES) // LANES
    n_exp = router_w.shape[-1]
    fox_row, diff_row, mla_row = cache_fox.shape[-1], cache_diff.shape[-1], cache_mla.shape[-1]
    n_pages = page_table.shape[1]
    past_len = n_pages * PAGE
    assert nb * dec_seq == TOKEN_TILE and dec_seq == 1 and kv_rank == LANES and cache_fox.shape[2] == PAGE
    assert seq % ATTN_TILE == 0 and n_pages % DECODE_PAGES == 0 and n_fox == n_mla == 2 * n_diff
    tp = n_seq * seq
    t = tp + TOKEN_TILE
    n_prompt_tiles = tp // TOKEN_TILE
    tiles_per_seq = seq // TOKEN_TILE

    sizes = (n_fox * HEAD64, HEAD64, HEAD64, n_fox, n_diff * LANES, LANES, LANES, q_rank, kv_rank, HEAD64)
    w_in_p, off = _permute_w_in(w_in, sizes)
    pos = jnp.concatenate([jnp.tile(jnp.arange(seq), n_seq), jnp.full((TOKEN_TILE,), past_len)])
    tabs = _rope_tables(pos)
    ln = jnp.arange(LANES)
    b64 = (ln[:, None] // HEAD64 == ln[None, :] // HEAD64).astype(BF16)
    tri_incl = (ln[None, :] <= ln[:, None]).astype(BF16)
    tri_strict = (ln[None, :] < ln[:, None]).astype(BF16)
    uext = jnp.concatenate([(ln[:, None] > ln[None, :]).astype(BF16), jnp.ones((LANES, LANES), BF16)], axis=1)
    indn, indr = _indicator(n_mla, MLA_NOPE), _indicator(n_mla, HEAD64)
    consts = (b64, tri_incl, indn, indn.T, indr, indr.T)
    wuq_p = jnp.concatenate([mla_wuq[..., :MLA_NOPE].reshape(depth, q_rank, -1),
                             mla_wuq[..., MLA_NOPE:].reshape(depth, q_rank, -1)], axis=-1)
    wuk_t = jnp.transpose(mla_wuk, (0, 2, 3, 1))
    wuv_h = jnp.transpose(mla_wuv, (0, 2, 1, 3)).astype(BF16)
    rw_hi = router_w.astype(BF16)
    rw_lo = (router_w - rw_hi.astype(F32)).astype(BF16)
    two = lambda g: jnp.concatenate([g, g], axis=-1)

    c_all = jnp.concatenate([c_sample, c_prompt], axis=0)
    mod = _ada_mod(c_all, ada_w, ada_b)
    mod4 = mod.reshape(depth, mod.shape[1], 1, mod.shape[2])

    x = jnp.concatenate([x_prompt.reshape(tp, d), x_sample.reshape(TOKEN_TILE, d)], axis=0)
    cfg = (n_prompt_tiles, tiles_per_seq, n_fox, n_diff, n_mla, q_rank, off)
    dims = dict(fox_row=fox_row, diff_row=diff_row, mla_row=mla_row)
    g_max = (t * TOP_K) // MOE_TILE + n_exp
    rows_out = []

    for l in range(depth):
        lam_init = 0.8 - 0.6 * math.exp(-0.3 * l)
        ms = _ModSpecs(l, d, n_prompt_tiles, tiles_per_seq, n_seq)
        h1 = _prenorm1(x, norm1_g[l][None], mod, mod4, ms)
        z = _matmul(h1, w_in_p, l, 640 if t % 640 == 0 else TOKEN_TILE, 1152 if w_in_p.shape[-1] % 1152 == 0 else LANES,
                    F32, "in_proj")
        gains = (two(fox_qn[l])[None], jnp.concatenate([fox_kn[l], jnp.ones((HEAD64,), F32)])[None],
                 two(diff_qn[l])[None], two(diff_kn[l])[None], mla_cqn[l][None], mla_ckvn[l][None],
                 two(mla_krn[l])[None], mla_qn[l][None, :MLA_NOPE], two(mla_qn[l][MLA_NOPE:])[None], fox_fb[l][None])
        pp = _prep(z, tabs, consts, gains, wuq_p[l], wuk_t[l], cfg, dims)
        rows_out.append((pp["frow"], pp["drow"], pp["mrow"]))

        nq = seq // ATTN_TILE
        cum_t = jnp.transpose(pp["cum"][:tp].reshape(n_seq, nq, ATTN_TILE, n_fox), (0, 1, 3, 2))
        of, od, om = _prompt_attention(pp, cum_t, diff_lam[l], diff_subln[l][None], wuv_h[l], lam_init,
                                       n_seq, seq, (n_fox, n_diff, n_mla))

        s_ = slice(tp, t)
        qf_s = jnp.pad(pp["qf"][s_].reshape(nb, n_fox, HEAD64), ((0, 0), (0, 0), (0, HEAD64)))
        qd_s = pp["qd"][s_].reshape(nb, n_diff, 2, HEAD64)
        zq = jnp.zeros_like(qd_s[:, :, 0])
        qd_s = jnp.concatenate([jnp.concatenate([qd_s[:, :, 0], zq], axis=-1),
                                jnp.concatenate([zq, qd_s[:, :, 1]], axis=-1)], axis=1)
        qlat_s = pp["qlat"][s_].reshape(nb, n_mla, LANES)
        qrope_s = pp["qrope"][s_].reshape(nb, n_mla, HEAD64)
        o_f, o_d, o_l = _decode_attention(l, page_table, qf_s, qd_s, qlat_s, qrope_s,
                                          pp["frow"][s_][:, None], pp["drow"][s_][:, None], pp["mrow"][s_][:, None],
                                          pp["frow"][s_][:, LANES:, None], uext, cache_fox, cache_diff, cache_mla)
        o_s = _sample_merge(o_f[:, :, HEAD64:].reshape(nb, -1), o_d[:, :n_diff].reshape(nb, -1),
                            o_d[:, n_diff:].reshape(nb, -1), o_l.reshape(nb, -1),
                            diff_lam[l], diff_subln[l][None], wuv_h[l], lam_init, d)

        o_all = jnp.concatenate([jnp.concatenate([of, od, om], axis=-1), o_s], axis=0)
        ao = _matmul(o_all, w_out, l, 640 if t % 640 == 0 else TOKEN_TILE, 1024, F32, "out_proj")

        x1, h2, top_i, gates, rank, counts = _prenorm2(x, ao, norm2_g[l][None], mod, mod4, ms,
                                                      rw_hi[l], rw_lo[l], router_b[l][None], tri_strict)

        cnt = counts[0].astype(I32)
        tiles_e = (cnt + MOE_TILE - 1) // MOE_TILE
        tile_end = jnp.cumsum(tiles_e)
        n_tiles = tile_end[-1]
        offs = (tile_end - tiles_e) * MOE_TILE
        pos_rows = offs[top_i] + rank
        gi = jnp.arange(g_max)
        tile_expert = jnp.searchsorted(tile_end, jnp.minimum(gi, n_tiles - 1), side="right").astype(I32)
        src = jnp.zeros((g_max * MOE_TILE,), I32).at[pos_rows.reshape(-1)].set(
            jnp.repeat(jnp.arange(t, dtype=I32), TOP_K))
        y = _moe_grouped(l, tile_expert, n_tiles.reshape(1).astype(I32), src, h2, moe_w_gu, moe_b_gu, moe_w_dn, moe_b_dn)
        x = _combine(pos_rows.astype(I32), x1, gates, y, mod, mod4, ms)

    def stack(k, lo, hi, shape):
        return jnp.stack([r[k][lo:hi].reshape(shape + (r[k].shape[-1],)) for r in rows_out])

    return (x[:tp].reshape(n_seq, seq, d), x[tp:].reshape(nb, dec_seq, d),
            stack(0, 0, tp, (n_seq, seq)), stack(1, 0, tp, (n_seq, seq)), stack(2, 0, tp, (n_seq, seq)),
            stack(0, tp, t, (nb, dec_seq)), stack(1, tp, t, (nb, dec_seq)), stack(2, tp, t, (nb, dec_seq)))
```

```python
import functools
import math

import jax
import jax.numpy as jnp
from jax import lax
from jax.experimental import pallas as pl
from jax.experimental.pallas import tpu as pltpu

F32 = jnp.float32
BF16 = jnp.bfloat16
I32 = jnp.int32

LANES = 128
SUBLANES = 8
VMEM_BYTES_V7X = 64 * 1024 * 1024
VMEM_LIMIT = 56 * 1024 * 1024

HEAD64 = 64
MLA_NOPE = 128
MLA_QK = MLA_NOPE + HEAD64
TOP_K = 4
SWIGLU_ALPHA = 1.702
SWIGLU_LIMIT = 7.0
ROPE_THETA = 10000.0
NORM_EPS = 1e-6
NEG_BIG = -1e30

TOKEN_TILE = 128
PAGE = 128
ATTN_TILE = 256
MOE_TILE = 256
DECODE_PAGES = 32


def _cparams(sem, vmem=VMEM_LIMIT):
    return pltpu.CompilerParams(dimension_semantics=sem, vmem_limit_bytes=vmem)


def _dot(a, b):
    return jnp.dot(a, b, preferred_element_type=F32)


def _dot_nt(a, b):
    return lax.dot_general(a, b, (((1,), (1,)), ((), ())), preferred_element_type=F32)


def _split2(x):
    hi = x.astype(BF16)
    lo = (x - hi.astype(F32)).astype(BF16)
    return hi, lo


def _split3(x):
    hi = x.astype(BF16)
    r = x - hi.astype(F32)
    mid = r.astype(BF16)
    lo = (r - mid.astype(F32)).astype(BF16)
    return hi, mid, lo


def _dot2(x, m):
    hi, lo = _split2(x)
    return _dot(hi, m) + _dot(lo, m)


def _lane(shape):
    return lax.broadcasted_iota(I32, shape, len(shape) - 1)


def _ada_kernel(c_ref, w_ref, b_ref, o_ref):
    c = c_ref[...]
    a = (c * jax.nn.sigmoid(c)).astype(BF16)
    o_ref[...] = _dot(a, w_ref[...].astype(BF16)) + b_ref[...]


def _ada_mod(c_all, ada_w, ada_b):
    depth, d, n = ada_w.shape
    r = c_all.shape[0]
    tn = 512
    return pl.pallas_call(
        _ada_kernel,
        grid=(depth, n // tn),
        in_specs=[
            pl.BlockSpec((r, d), lambda l, j: (0, 0)),
            pl.BlockSpec((None, d, tn), lambda l, j: (l, 0, j)),
            pl.BlockSpec((None, 1, tn), lambda l, j: (l, 0, j)),
        ],
        out_specs=pl.BlockSpec((None, r, tn), lambda l, j: (l, 0, j)),
        out_shape=jax.ShapeDtypeStruct((depth, r, n), F32),
        compiler_params=_cparams(("arbitrary", "arbitrary")),
        name="ada_mod",
    )(c_all, ada_w, ada_b.reshape(depth, 1, n))


class _ModSpecs:
    def __init__(self, layer, d, n_prompt_tiles, tiles_per_seq, n_seq):
        self.layer, self.d = layer, d
        self.n_prompt_tiles, self.tiles_per_seq, self.n_seq = n_prompt_tiles, tiles_per_seq, n_seq

    def sample(self, piece):
        l = self.layer
        return pl.BlockSpec((None, TOKEN_TILE, self.d), lambda i: (l, 0, piece))

    def prompt(self, piece):
        l, tps, last = self.layer, self.tiles_per_seq, self.n_seq - 1
        return pl.BlockSpec((None, None, 1, self.d),
                            lambda i: (l, TOKEN_TILE + jnp.minimum(i // tps, last), 0, piece))


def _pick(is_sample, sample_ref, prompt_ref):
    return jnp.where(is_sample, sample_ref[...], prompt_ref[...])


def _rms(x):
    return x * lax.rsqrt(jnp.mean(x * x, axis=-1, keepdims=True) + NORM_EPS)


def _prenorm1_kernel(n_prompt_tiles, x_ref, g_ref, shs_ref, shp_ref, scs_ref, scp_ref, h_ref):
    is_sample = pl.program_id(0) >= n_prompt_tiles
    shift = _pick(is_sample, shs_ref, shp_ref)
    scale = _pick(is_sample, scs_ref, scp_ref)
    h_ref[...] = (_rms(x_ref[...]) * g_ref[...] * (1.0 + scale) + shift).astype(h_ref.dtype)


def _prenorm1(x, norm_g, mod, mod4, ms):
    t, d = x.shape
    tile = pl.BlockSpec((TOKEN_TILE, d), lambda i: (i, 0))
    return pl.pallas_call(
        functools.partial(_prenorm1_kernel, ms.n_prompt_tiles),
        grid=(t // TOKEN_TILE,),
        in_specs=[tile, pl.BlockSpec((1, d), lambda i: (0, 0)),
                  ms.sample(0), ms.prompt(0), ms.sample(1), ms.prompt(1)],
        out_specs=tile,
        out_shape=jax.ShapeDtypeStruct((t, d), BF16),
        compiler_params=_cparams(("arbitrary",)),
        name="prenorm1",
    )(x, norm_g, mod, mod4, mod, mod4)


def _mm_kernel(a_ref, w_ref, o_ref, wb_ref):
    @pl.when(pl.program_id(1) == 0)
    def _():
        wb_ref[...] = w_ref[...].astype(BF16)

    o_ref[...] = _dot(a_ref[...], wb_ref[...]).astype(o_ref.dtype)


def _matmul(a, w, layer, tm, tn, out_dtype, name):
    m, k = a.shape
    n = w.shape[-1]
    assert m % tm == 0 and n % tn == 0
    return pl.pallas_call(
        _mm_kernel,
        grid=(n // tn, m // tm),
        in_specs=[pl.BlockSpec((tm, k), lambda j, i: (i, 0)),
                  pl.BlockSpec((None, k, tn), lambda j, i: (layer, 0, j), pipeline_mode=pl.Buffered(1))],
        out_specs=pl.BlockSpec((tm, tn), lambda j, i: (i, j)),
        out_shape=jax.ShapeDtypeStruct((m, n), out_dtype),
        scratch_shapes=[pltpu.VMEM((k, tn), BF16)],
        compiler_params=_cparams(("arbitrary", "arbitrary")),
        name=name,
    )(a, w)


def _rope(x, c, sa, sb):
    return x * c + pltpu.roll(x, 32, 1) * sa + pltpu.roll(x, 96, 1) * sb


def _rms64(x, b64):
    return x * lax.rsqrt(_dot2(x * x, b64) * (1.0 / HEAD64) + NORM_EPS)


def _prep_kernel(cfg, z_ref, c_ref, sa_ref, sb_ref, b64_ref, tri_ref,
                 indn_ref, indnt_ref, indr_ref, indrt_ref,
                 fqn_ref, fkn_ref, dqn_ref, dkn_ref, cqn_ref, ckvn_ref, krn_ref, qnn_ref, qnr_ref, fb_ref,
                 wuq_ref, wuk_ref,
                 qf_ref, qd_ref, qlat_ref, qrope_ref, frow_ref, drow_ref, mrow_ref,
                 fkk_ref, fvv_ref, dk_ref, dv_ref, mk_ref, cum_ref,
                 wuq_bf, wuk_bf, carry_ref):
    (n_prompt_tiles, tiles_per_seq, n_fox, n_diff, n_mla, q_rank, off) = cfg
    i = pl.program_id(0)
    is_sample = i >= n_prompt_tiles
    tm = TOKEN_TILE

    @pl.when(i == 0)
    def _():
        wuq_bf[...] = wuq_ref[...].astype(BF16)
        wuk_bf[...] = wuk_ref[...].astype(BF16)

    c, sa, sb = c_ref[...], sa_ref[...], sb_ref[...]
    b64 = b64_ref[...]
    lane = _lane((tm, LANES))
    left = lane < HEAD64

    for s in range(n_fox // 2):
        x = z_ref[:, off["fq"] + s * LANES: off["fq"] + (s + 1) * LANES]
        qf_ref[:, s * LANES:(s + 1) * LANES] = (_rms64(x, b64) * (fqn_ref[...] * 0.125)).astype(BF16)

    for s in range(n_diff):
        x = z_ref[:, off["dq"] + s * LANES: off["dq"] + (s + 1) * LANES]
        y = _rope(_rms64(x, b64) * dqn_ref[...], c, sa, sb)
        qd_ref[:, s * LANES:(s + 1) * LANES] = (y * 0.125).astype(BF16)

    kv = z_ref[:, off["fkv"]: off["fkv"] + LANES]
    kvn = jnp.where(left, _rms64(kv, b64) * fkn_ref[...], kv)
    kv_sw = pltpu.roll(kvn, HEAD64, 1)
    fkk_ref[...] = jnp.where(left, kvn, kv_sw).astype(BF16)
    fvv_ref[...] = jnp.where(left, kv_sw, kvn).astype(BF16)
    tail = z_ref[:, off["tail"]: off["tail"] + LANES]
    ff = tail[:, HEAD64:HEAD64 + n_fox] + fb_ref[...]
    logf = jnp.minimum(ff, 0.0) - jnp.log(1.0 + jnp.exp(-jnp.abs(ff)))
    frow_ref[:, 0:LANES] = kvn
    frow_ref[:, LANES:LANES + n_fox] = logf

    @pl.when(i % tiles_per_seq == 0)
    def _():
        carry_ref[...] = jnp.zeros_like(carry_ref)

    hi, mid, lo = _split3(logf)
    tri = tri_ref[...]
    cum = carry_ref[...] + _dot(tri, hi) + _dot(tri, mid) + _dot(tri, lo)
    carry_ref[...] = cum[tm - 1:tm, :]
    cum_ref[...] = jnp.where(is_sample, logf, cum)

    dk = _rope(_rms64(z_ref[:, off["dk"]: off["dk"] + LANES], b64) * dkn_ref[...], c, sa, sb)
    dv = z_ref[:, off["dv"]: off["dv"] + LANES]
    drow_ref[:, 0:LANES] = dk
    drow_ref[:, LANES:2 * LANES] = dv
    dk_ref[...] = dk.astype(BF16)
    dv_ref[...] = dv.astype(BF16)

    ckv = _rms(z_ref[:, off["ckv"]: off["ckv"] + LANES]) * ckvn_ref[...]
    kr = _rope(_rms64(tail, b64) * krn_ref[...], c, sa, sb)
    mrow_ref[:, 0:LANES] = ckv
    mrow_ref[:, LANES:LANES + HEAD64] = kr[:, 0:HEAD64]
    mk_ref[:, 0:LANES] = ckv.astype(BF16)
    mk_ref[:, LANES:2 * LANES] = jnp.where(left, kr, pltpu.roll(kr, HEAD64, 1)).astype(BF16)

    cq = (_rms(z_ref[:, off["cq"]: off["cq"] + q_rank]) * cqn_ref[...]).astype(BF16)
    q = _dot(cq, wuq_bf[...])
    nn = n_mla * MLA_NOPE
    q_n, q_r = q[:, :nn], q[:, nn:]
    ss = _dot2(q_n * q_n, indn_ref[...]) + _dot2(q_r * q_r, indr_ref[...])
    r = lax.rsqrt(ss * (1.0 / MLA_QK) + NORM_EPS)
    mla_scale = MLA_QK ** -0.5
    r_n = _dot2(r, indnt_ref[...])
    for h in range(n_mla):
        qh = (q_n[:, h * LANES:(h + 1) * LANES] * r_n[:, h * LANES:(h + 1) * LANES] * qnn_ref[...]).astype(BF16)
        qlat_ref[:, h * LANES:(h + 1) * LANES] = (_dot(qh, wuk_bf[h]) * mla_scale).astype(BF16)
    r_r = _dot2(r, indrt_ref[...])
    for s in range(n_mla // 2):
        sl = slice(s * LANES, (s + 1) * LANES)
        y = _rope(q_r[:, sl] * r_r[:, sl] * qnr_ref[...], c, sa, sb)
        qrope_ref[:, sl] = (y * mla_scale).astype(BF16)


def _prep(z, tabs, consts, gains, wuq_p, wuk_t, cfg, dims):
    t, zc = z.shape
    (n_prompt_tiles, tiles_per_seq, n_fox, n_diff, n_mla, q_rank, off) = cfg
    tm = TOKEN_TILE
    row = lambda w: pl.BlockSpec((tm, w), lambda i: (i, 0))
    full = lambda a: pl.BlockSpec(a.shape, lambda i: (0,) * a.ndim)
    in_arrays = [z, *tabs, *consts, *gains, wuq_p, wuk_t]
    in_specs = [row(zc), row(LANES), row(LANES), row(LANES)] + [full(a) for a in in_arrays[4:]]
    widths = [("qf", n_fox * HEAD64, BF16), ("qd", n_diff * LANES, BF16), ("qlat", n_mla * LANES, BF16),
              ("qrope", n_mla * HEAD64, BF16), ("frow", dims["fox_row"], F32), ("drow", dims["diff_row"], F32),
              ("mrow", dims["mla_row"], F32), ("fkk", LANES, BF16), ("fvv", LANES, BF16), ("dk", LANES, BF16),
              ("dv", LANES, BF16), ("mk", 2 * LANES, BF16), ("cum", n_fox, F32)]
    outs = pl.pallas_call(
        functools.partial(_prep_kernel, cfg),
        grid=(t // tm,),
        in_specs=in_specs,
        out_specs=[row(w) for _, w, _ in widths],
        out_shape=[jax.ShapeDtypeStruct((t, w), dt) for _, w, dt in widths],
        scratch_shapes=[pltpu.VMEM(wuq_p.shape, BF16), pltpu.VMEM(wuk_t.shape, BF16), pltpu.VMEM((1, n_fox), F32)],
        compiler_params=_cparams(("arbitrary",)),
        name="qkv_prep",
    )(*in_arrays)
    return {k: v for (k, _, _), v in zip(widths, outs)}


def _online_update(s, m, l):
    m_new = jnp.maximum(m, jnp.max(s, axis=-1, keepdims=True))
    p = jnp.exp(s - m_new)
    alpha = jnp.exp(m - m_new)
    return m_new, alpha * l + jnp.sum(p, axis=-1, keepdims=True), alpha, p


def _causal_mask(tq):
    r = lax.broadcasted_iota(I32, (tq, tq), 0)
    c = lax.broadcasted_iota(I32, (tq, tq), 1)
    return c <= r


def _fox_attn_kernel(n_heads, q_ref, kk_ref, vv_ref, cq_ref, ckt_ref, o_ref):
    tq = ATTN_TILE
    i = pl.program_id(1)
    lane = _lane((tq, LANES))
    left = lane < HEAD64
    mask = _causal_mask(tq)
    one = (jnp.full((tq, 1), NEG_BIG, F32), jnp.zeros((tq, 1), F32), jnp.zeros((tq, LANES), F32))
    for s in range(n_heads // 2):
        qs = q_ref[:, s * LANES:(s + 1) * LANES]
        qpair = (jnp.where(left, qs, jnp.zeros_like(qs)), jnp.where(left, jnp.zeros_like(qs), qs))
        cqpair = (cq_ref[:, 2 * s:2 * s + 1], cq_ref[:, 2 * s + 1:2 * s + 2])

        def step(j, carry, masked, qpair=qpair, cqpair=cqpair, s=s):
            rows = pl.ds(pl.multiple_of(j * tq, tq), tq)
            k = kk_ref[rows, :]
            v = vv_ref[rows, :]
            new = []
            for half in range(2):
                m, l, acc = carry[half]
                sc = _dot_nt(qpair[half], k) + cqpair[half] - ckt_ref[j, 2 * s + half:2 * s + half + 1, :]
                if masked:
                    sc = jnp.where(mask, sc, NEG_BIG)
                m, l, alpha, p = _online_update(sc, m, l)
                new.append((m, l, alpha * acc + _dot(p.astype(BF16), v)))
            return tuple(new)

        carry = lax.fori_loop(0, i, lambda j, c: step(j, c, False), (one, one))
        (m0, l0, a0), (m1, l1, a1) = step(i, carry, True)
        o_ref[:, s * LANES:(s + 1) * LANES] = jnp.where(left, a0 / l0, a1 / l1).astype(o_ref.dtype)


def _diff_lambda(lam_ref, lam_init):
    lv = lam_ref[...]
    a = jnp.sum(jnp.sum(lv[0:1] * lv[1:2], axis=1, keepdims=True), axis=0, keepdims=True)
    b = jnp.sum(jnp.sum(lv[2:3] * lv[3:4], axis=1, keepdims=True), axis=0, keepdims=True)
    return jnp.exp(a) - jnp.exp(b) + lam_init


def _diff_attn_kernel(n_heads, lam_init, q_ref, k_ref, v_ref, lam_ref, sub_ref, o_ref):
    tq = ATTN_TILE
    i = pl.program_id(1)
    lane = _lane((tq, LANES))
    left = lane < HEAD64
    mask = _causal_mask(tq)
    lam = _diff_lambda(lam_ref, lam_init)
    for h in range(n_heads):
        qs = q_ref[:, h * LANES:(h + 1) * LANES]
        q1 = jnp.where(left, qs, jnp.zeros_like(qs))
        q2 = jnp.where(left, jnp.zeros_like(qs), qs)

        def step(j, carry, masked, q1=q1, q2=q2):
            rows = pl.ds(pl.multiple_of(j * tq, tq), tq)
            k = k_ref[rows, :]
            v = v_ref[rows, :]
            new = []
            for q, (m, l, acc) in zip((q1, q2), carry):
                sc = _dot_nt(q, k)
                if masked:
                    sc = jnp.where(mask, sc, NEG_BIG)
                m, l, alpha, p = _online_update(sc, m, l)
                new.append((m, l, alpha * acc + _dot(p.astype(BF16), v)))
            return tuple(new)

        one = (jnp.full((tq, 1), NEG_BIG, F32), jnp.zeros((tq, 1), F32), jnp.zeros((tq, LANES), F32))
        carry = lax.fori_loop(0, i, lambda j, c: step(j, c, False), (one, one))
        (m1, l1, a1), (m2, l2, a2) = step(i, carry, True)
        od = a1 / l1 - lam * (a2 / l2)
        o_ref[:, h * LANES:(h + 1) * LANES] = (_rms(od) * sub_ref[...] * (1.0 - lam_init)).astype(o_ref.dtype)


def _mla_attn_kernel(n_heads, qlat_ref, qrope_ref, k_ref, wuv_ref, o_ref):
    tq = ATTN_TILE
    i = pl.program_id(1)
    lane = _lane((tq, LANES))
    left = lane < HEAD64
    mask = _causal_mask(tq)
    one = (jnp.full((tq, 1), NEG_BIG, F32), jnp.zeros((tq, 1), F32), jnp.zeros((tq, LANES), F32))
    for s in range(n_heads // 2):
        qr = qrope_ref[:, s * LANES:(s + 1) * LANES]
        qpair = tuple(
            jnp.concatenate([qlat_ref[:, (2 * s + half) * LANES:(2 * s + half + 1) * LANES],
                             jnp.where(left if half == 0 else jnp.logical_not(left), qr, jnp.zeros_like(qr))], axis=-1)
            for half in range(2))

        def step(j, carry, masked, qpair=qpair):
            k = k_ref[pl.ds(pl.multiple_of(j * tq, tq), tq), :]
            new = []
            for half in range(2):
                m, l, acc = carry[half]
                sc = _dot_nt(qpair[half], k)
                if masked:
                    sc = jnp.where(mask, sc, NEG_BIG)
                m, l, alpha, p = _online_update(sc, m, l)
                new.append((m, l, alpha * acc + _dot(p.astype(BF16), k[:, 0:LANES])))
            return tuple(new)

        carry = lax.fori_loop(0, i, lambda j, c: step(j, c, False), (one, one))
        for half, (m, l, acc) in enumerate(step(i, carry, True)):
            h = 2 * s + half
            o_ref[:, h * LANES:(h + 1) * LANES] = _dot((acc / l).astype(BF16), wuv_ref[h]).astype(o_ref.dtype)


def _prompt_attention(pp, cum_t, diff_lam, diff_subln, wuv_h, lam_init, n_seq, seq, heads):
    n_fox, n_diff, n_mla = heads
    tq = ATTN_TILE
    nq = seq // tq
    tp = n_seq * seq
    qrow = lambda w: pl.BlockSpec((tq, w), lambda b, i: (b * nq + i, 0))
    seqrows = lambda w: pl.BlockSpec((seq, w), lambda b, i: (b, 0))
    full = lambda a: pl.BlockSpec(a.shape, lambda b, i: (0,) * a.ndim)
    cp = _cparams(("arbitrary", "arbitrary"))
    of = pl.pallas_call(
        functools.partial(_fox_attn_kernel, n_fox),
        grid=(n_seq, nq),
        in_specs=[qrow(n_fox * HEAD64), seqrows(LANES), seqrows(LANES), qrow(n_fox),
                  pl.BlockSpec((None, nq, n_fox, tq), lambda b, i: (b, 0, 0, 0))],
        out_specs=qrow(n_fox * HEAD64),
        out_shape=jax.ShapeDtypeStruct((tp, n_fox * HEAD64), BF16),
        compiler_params=cp, name="fox_attn",
    )(pp["qf"], pp["fkk"], pp["fvv"], pp["cum"], cum_t)
    od = pl.pallas_call(
        functools.partial(_diff_attn_kernel, n_diff, lam_init),
        grid=(n_seq, nq),
        in_specs=[qrow(n_diff * LANES), seqrows(LANES), seqrows(LANES), full(diff_lam), full(diff_subln)],
        out_specs=qrow(n_diff * LANES),
        out_shape=jax.ShapeDtypeStruct((tp, n_diff * LANES), BF16),
        compiler_params=cp, name="diff_attn",
    )(pp["qd"], pp["dk"], pp["dv"], diff_lam, diff_subln)
    om = pl.pallas_call(
        functools.partial(_mla_attn_kernel, n_mla),
        grid=(n_seq, nq),
        in_specs=[qrow(n_mla * LANES), qrow(n_mla * HEAD64), seqrows(2 * LANES), full(wuv_h)],
        out_specs=qrow(n_mla * LANES),
        out_shape=jax.ShapeDtypeStruct((tp, n_mla * LANES), BF16),
        compiler_params=cp, name="mla_attn",
    )(pp["qlat"], pp["qrope"], pp["mk"], wuv_h)
    return of, od, om


def _page_copy(cache_ref, buf_ref, sem_ref, layer, page, slot, g):
    return pltpu.make_async_copy(cache_ref.at[layer, page], buf_ref.at[slot, g], sem_ref.at[slot])


def _decode_kernel(cfg, pt_ref, qf_ref, qd_ref, qm_ref, fnew_ref, dnew_ref, mnew_ref, lfnew_ref, uext_ref,
                   cf_ref, cd_ref, cm_ref,
                   of_ref, od_ref, ol_ref,
                   fbuf, dbuf, mbuf, sem_f, sem_d, sem_m, st_ref, accf_ref, accd_ref, accm_ref, carry_ref):
    (layer, n_pages, n_chunks, n_heads) = cfg
    gpp = DECODE_PAGES
    t = pl.program_id(0)
    n_steps = pl.num_programs(0)
    b = t // n_chunks
    c = t % n_chunks
    slot = t % 2

    def issue(step, sl, start):
        sb = step // n_chunks
        sc = step % n_chunks
        base = sb * n_pages + (n_chunks - 1 - sc) * gpp
        for g in range(gpp):
            page = pt_ref[base + g]
            for cache, buf, sem in ((cf_ref, fbuf, sem_f), (cd_ref, dbuf, sem_d), (cm_ref, mbuf, sem_m)):
                cp = _page_copy(cache, buf, sem, layer, page, sl, g)
                if start:
                    cp.start()
                else:
                    cp.wait()

    @pl.when(t == 0)
    def _():
        issue(t, slot, True)


    qf = qf_ref[...]
    qd = qd_ref[...]
    qm = qm_ref[...]
    hh = n_heads
    accs = (accf_ref, accd_ref, accm_ref)

    @pl.when(c == 0)
    def _():
        rnd = lambda v: v.astype(BF16).astype(F32)
        fnew, dnew, mnew = fnew_ref[...], dnew_ref[...], mnew_ref[...]
        s_f = jnp.sum(qf.astype(F32) * rnd(fnew[:, 0:HEAD64]), axis=-1, keepdims=True)
        s_d = jnp.sum(qd.astype(F32) * rnd(dnew[:, 0:LANES]), axis=-1, keepdims=True)
        s_m = jnp.sum(qm.astype(F32) * rnd(mnew), axis=-1, keepdims=True)
        values = (fnew[:, HEAD64:LANES], dnew[:, LANES:2 * LANES], mnew[:, 0:LANES])
        for k, (sv, vrow) in enumerate(zip((s_f, s_d, s_m), values)):
            st_ref[2 * k + 0] = sv * jnp.ones((hh, LANES), F32)
            st_ref[2 * k + 1] = jnp.ones((hh, LANES), F32)
            accs[k][...] = rnd(vrow) * jnp.ones((hh, 1), F32)
        carry_ref[...] = jnp.zeros_like(carry_ref)

    issue(t, slot, False)
    nxt = jnp.minimum(t + 1, n_steps - 1)
    issue(nxt, 1 - slot, True)

    def update(k, sc, pv_fn):
        m, l = st_ref[2 * k + 0][:, 0:1], st_ref[2 * k + 1][:, 0:1]
        m_new = jnp.maximum(m, jnp.max(sc, axis=-1, keepdims=True))
        p = jnp.exp(sc - m_new).astype(BF16)
        alpha = jnp.exp(m - m_new)
        st_ref[2 * k + 0] = m_new * jnp.ones((hh, LANES), F32)
        st_ref[2 * k + 1] = (alpha * l + jnp.sum(p.astype(F32), axis=-1, keepdims=True)) * jnp.ones((hh, LANES), F32)
        pv = pv_fn(0, p[:, 0:PAGE])
        for g in range(1, gpp):
            pv = pv + pv_fn(g, p[:, g * PAGE:(g + 1) * PAGE])
        accs[k][...] = alpha * accs[k][...] + pv

    parts = []
    for g in range(gpp):
        parts.extend(_split3(fbuf[slot, g, LANES:LANES + hh, :]))
    sfx = _dot(jnp.concatenate(parts, axis=0), uext_ref[...])
    decay = [None] * gpp
    run = carry_ref[...]
    lfnew = lfnew_ref[...]
    for g in reversed(range(gpp)):
        r0 = 3 * hh * g
        sg = sfx[r0:r0 + hh] + sfx[r0 + hh:r0 + 2 * hh] + sfx[r0 + 2 * hh:r0 + 3 * hh]
        decay[g] = (run + lfnew) + sg[:, 0:LANES]
        run = run + sg[:, LANES:2 * LANES]
    carry_ref[...] = run
    s_f = jnp.concatenate([_dot(qf, fbuf[slot, g, 0:HEAD64, :].astype(BF16)) + decay[g] for g in range(gpp)], axis=1)
    update(0, s_f, lambda g, p: _dot_nt(p, fbuf[slot, g, HEAD64:LANES, :].astype(BF16)))

    s_d = jnp.concatenate([_dot_nt(qd, dbuf[slot, g, :, 0:LANES].astype(BF16)) for g in range(gpp)], axis=1)
    update(1, s_d, lambda g, p: _dot(p, dbuf[slot, g, :, LANES:2 * LANES].astype(BF16)))

    s_m = jnp.concatenate([_dot(qm, mbuf[slot, g].astype(BF16)) for g in range(gpp)], axis=1)
    update(2, s_m, lambda g, p: _dot_nt(p, mbuf[slot, g, 0:LANES, :].astype(BF16)))

    @pl.when(c == n_chunks - 1)
    def _():
        of_ref[...] = accf_ref[...] / st_ref[1][:, 0:HEAD64]
        od_ref[...] = accd_ref[...] / st_ref[3]
        ol_ref[...] = accm_ref[...] / st_ref[5]

    @pl.when(t == n_steps - 1)
    def _():
        issue(nxt, 1 - slot, False)


def _decode_attention(layer, page_table, qf, qd, qm, fnew, dnew, mnew, lfnew, uext, cache_fox_t, cache_diff, cache_mla_t):
    nb, n_pages = page_table.shape
    hh = qf.shape[1]
    gpp = DECODE_PAGES
    n_chunks = n_pages // gpp
    fw, dw, mw = cache_fox_t.shape[2], cache_diff.shape[3], cache_mla_t.shape[2]
    cfg = (layer, n_pages, n_chunks, hh)
    per_b = lambda a: pl.BlockSpec((None,) + a.shape[1:], lambda t, pt: (t // n_chunks,) + (0,) * (a.ndim - 1))
    anyspec = pl.BlockSpec(memory_space=pl.ANY)
    out = lambda w: jax.ShapeDtypeStruct((nb, hh, w), F32)
    out_spec = lambda w: pl.BlockSpec((None, hh, w), lambda t, pt: (t // n_chunks, 0, 0))
    grid_spec = pltpu.PrefetchScalarGridSpec(
        num_scalar_prefetch=1,
        grid=(nb * n_chunks,),
        in_specs=[per_b(qf), per_b(qd), per_b(qm), per_b(fnew), per_b(dnew), per_b(mnew), per_b(lfnew),
                  pl.BlockSpec(uext.shape, lambda t, pt: (0, 0)), anyspec, anyspec, anyspec],
        out_specs=[out_spec(HEAD64), out_spec(LANES), out_spec(LANES)],
        scratch_shapes=[pltpu.VMEM((2, gpp, fw, PAGE), F32), pltpu.VMEM((2, gpp, PAGE, dw), F32),
                        pltpu.VMEM((2, gpp, mw, PAGE), F32),
                        pltpu.SemaphoreType.DMA((2,)), pltpu.SemaphoreType.DMA((2,)), pltpu.SemaphoreType.DMA((2,)),
                        pltpu.VMEM((6, hh, LANES), F32), pltpu.VMEM((hh, HEAD64), F32), pltpu.VMEM((hh, LANES), F32),
                        pltpu.VMEM((hh, LANES), F32), pltpu.VMEM((hh, LANES), F32)],
    )
    return pl.pallas_call(
        functools.partial(_decode_kernel, cfg),
        grid_spec=grid_spec,
        out_shape=[out(HEAD64), out(LANES), out(LANES)],
        compiler_params=_cparams(("arbitrary",)),
        name="decode_attn",
    )(page_table.reshape(-1), qf, qd, qm, fnew, dnew, mnew, lfnew, uext, cache_fox_t, cache_diff, cache_mla_t)


def _sample_merge_kernel(n_diff, n_mla, lam_init, of_ref, o1_ref, o2_ref, ol_ref, lam_ref, sub_ref, wuv_ref, o_ref):
    lam = _diff_lambda(lam_ref, lam_init)
    nf = of_ref.shape[1]
    o_ref[:, 0:nf] = of_ref[...].astype(o_ref.dtype)
    for h in range(n_diff):
        sl = slice(h * LANES, (h + 1) * LANES)
        od = o1_ref[:, sl] - lam * o2_ref[:, sl]
        o_ref[:, nf + h * LANES: nf + (h + 1) * LANES] = (_rms(od) * sub_ref[...] * (1.0 - lam_init)).astype(o_ref.dtype)
    base = nf + n_diff * LANES
    for h in range(n_mla):
        sl = slice(h * LANES, (h + 1) * LANES)
        o_ref[:, base + h * LANES: base + (h + 1) * LANES] = _dot(ol_ref[:, sl].astype(BF16), wuv_ref[h]).astype(o_ref.dtype)


def _sample_merge(of, o1, o2, ol, diff_lam, diff_subln, wuv_h, lam_init, d):
    nb = of.shape[0]
    n_diff = o1.shape[1] // LANES
    n_mla = ol.shape[1] // LANES
    args = (of, o1, o2, ol, diff_lam, diff_subln, wuv_h)
    return pl.pallas_call(
        functools.partial(_sample_merge_kernel, n_diff, n_mla, lam_init),
        grid=(1,),
        in_specs=[pl.BlockSpec(a.shape, lambda i, nd=a.ndim: (0,) * nd) for a in args],
        out_specs=pl.BlockSpec((nb, d), lambda i: (0, 0)),
        out_shape=jax.ShapeDtypeStruct((nb, d), BF16),
        compiler_params=_cparams(("arbitrary",)),
        name="sample_merge",
    )(*args)


def _prenorm2_kernel(n_prompt_tiles, n_exp, x_ref, ao_ref, g_ref, g1s_ref, g1p_ref, shs_ref, shp_ref, scs_ref, scp_ref,
                     rw_hi_ref, rw_lo_ref, rb_ref, tri_ref,
                     x1_ref, h_ref, ti_ref, gate_ref, rank_ref, cnt_ref, base_ref):
    i = pl.program_id(0)
    is_sample = i >= n_prompt_tiles
    tm = TOKEN_TILE

    @pl.when(i == 0)
    def _():
        base_ref[...] = jnp.zeros_like(base_ref)

    x1 = x_ref[...] + _pick(is_sample, g1s_ref, g1p_ref) * ao_ref[...]
    x1_ref[...] = x1
    h = _rms(x1) * g_ref[...] * (1.0 + _pick(is_sample, scs_ref, scp_ref)) + _pick(is_sample, shs_ref, shp_ref)
    h_ref[...] = h

    h_hi, h_lo = _split2(h)
    logits = _dot(h_hi, rw_hi_ref[...]) + _dot(h_lo, rw_hi_ref[...]) + _dot(h_hi, rw_lo_ref[...]) + rb_ref[...]
    lane = _lane((tm, n_exp))
    lane_k = _lane((tm, TOP_K))
    work = logits
    vals, sels = [], []
    top_i = jnp.zeros((tm, TOP_K), I32)
    for k in range(TOP_K):
        v = jnp.max(work, axis=-1, keepdims=True)
        idx = jnp.min(jnp.where(work == v, lane, n_exp), axis=-1, keepdims=True)
        sel = lane == idx
        vals.append(v)
        sels.append(sel)
        top_i = jnp.where(lane_k == k, idx, top_i)
        work = jnp.where(sel, -jnp.inf, work)
    es = [jnp.exp(v - vals[0]) for v in vals]
    tot = es[0] + es[1] + es[2] + es[3]
    gates = jnp.zeros((tm, TOP_K), F32)
    for k in range(TOP_K):
        gates = jnp.where(lane_k == k, es[k] / tot, gates)

    onehot = jnp.zeros((tm, n_exp), F32)
    for sel in sels:
        onehot = onehot + sel.astype(F32)
    before = _dot(tri_ref[...], onehot.astype(BF16)) + base_ref[...]
    rank = jnp.zeros((tm, TOP_K), F32)
    for k in range(TOP_K):
        rk = jnp.sum(jnp.where(sels[k], before, 0.0), axis=-1, keepdims=True)
        rank = jnp.where(lane_k == k, rk, rank)
    base_ref[...] = base_ref[...] + jnp.sum(onehot, axis=0, keepdims=True)
    ti_ref[...] = top_i
    gate_ref[...] = gates
    rank_ref[...] = rank.astype(I32)
    cnt_ref[...] = base_ref[...]


def _prenorm2(x, ao, norm_g, mod, mod4, ms, rw_hi, rw_lo, rb, tri_strict):
    t, d = x.shape
    n_exp = rw_hi.shape[1]
    tile = pl.BlockSpec((TOKEN_TILE, d), lambda i: (i, 0))
    small = pl.BlockSpec((TOKEN_TILE, TOP_K), lambda i: (i, 0))
    full = lambda a: pl.BlockSpec(a.shape, lambda i: (0,) * a.ndim)
    return pl.pallas_call(
        functools.partial(_prenorm2_kernel, ms.n_prompt_tiles, n_exp),
        grid=(t // TOKEN_TILE,),
        in_specs=[tile, tile, full(norm_g), ms.sample(2), ms.prompt(2), ms.sample(3), ms.prompt(3),
                  ms.sample(4), ms.prompt(4), full(rw_hi), full(rw_lo), full(rb), full(tri_strict)],
        out_specs=[tile, tile, small, small, small, pl.BlockSpec((1, n_exp), lambda i: (0, 0))],
        out_shape=[jax.ShapeDtypeStruct((t, d), F32), jax.ShapeDtypeStruct((t, d), F32),
                   jax.ShapeDtypeStruct((t, TOP_K), I32), jax.ShapeDtypeStruct((t, TOP_K), F32),
                   jax.ShapeDtypeStruct((t, TOP_K), I32), jax.ShapeDtypeStruct((1, n_exp), F32)],
        scratch_shapes=[pltpu.VMEM((1, n_exp), F32)],
        compiler_params=_cparams(("arbitrary",)),
        name="prenorm2_router",
    )(x, ao, norm_g, mod, mod4, mod, mod4, mod, mod4, rw_hi, rw_lo, rb, tri_strict)


def _row_gather(src_ref, idx_ref, base, n_rows, dst_of, sem, start):
    for n in range(n_rows):
        row = idx_ref[base + n] if start else 0
        cp = pltpu.make_async_copy(src_ref.at[pl.ds(row, 1), :], dst_of(n), sem)
        if start:
            cp.start()
        else:
            cp.wait()


def _moe_kernel(cfg, te_ref, nt_ref, src_ref, h_ref, wgu_ref, wdn_ref, bgu_ref, bdn_ref, y_ref,
                xbuf, wgu_bf, wdn_bf, stage_gu, stage_dn, sem_x, sem_w):
    (layer, d, f) = cfg
    tm = MOE_TILE
    g = pl.program_id(0)
    last = pl.num_programs(0) - 1
    slot = g % 2
    e = te_ref[g]
    active = g < nt_ref[0]
    prev = te_ref[jnp.maximum(g - 1, 0)]
    changed = jnp.logical_or(g == 0, e != prev)

    def gather(tile, sl, start):
        _row_gather(h_ref, src_ref, tile * tm, tm, lambda r: xbuf.at[sl, pl.ds(r, 1), :], sem_x.at[sl], start)

    @pl.when(g == 0)
    def _():
        gather(g, slot, True)

    gather(g, slot, False)
    nxt = jnp.minimum(g + 1, last)
    gather(nxt, 1 - slot, True)

    gu_rows, dn_rows = stage_gu.shape[1], stage_dn.shape[1]
    chunks = [(wgu_ref, stage_gu, wgu_bf, k * gu_rows, gu_rows) for k in range(d // gu_rows)]
    chunks += [(wdn_ref, stage_dn, wdn_bf, k * dn_rows, dn_rows) for k in range(f // dn_rows)]

    @pl.when(jnp.logical_and(active, changed))
    def _():
        copies = [pltpu.make_async_copy(w.at[layer, e, pl.ds(r0, nr), :], st.at[k % 2], sem_w.at[k % 2])
                  for k, (w, st, _, r0, nr) in enumerate(chunks)]
        copies[0].start()
        for k, (_, st, dst, r0, nr) in enumerate(chunks):
            if k + 1 < len(chunks):
                copies[k + 1].start()
            copies[k].wait()
            dst[r0:r0 + nr, :] = st[k % 2].astype(BF16)

    @pl.when(active)
    def _():
        x = xbuf[slot].astype(BF16)
        gu = _dot(x, wgu_bf[...]) + bgu_ref[...]
        gg = jnp.minimum(gu[:, :f], SWIGLU_LIMIT)
        uu = jnp.clip(gu[:, f:], -SWIGLU_LIMIT, SWIGLU_LIMIT)
        act = gg * jax.nn.sigmoid(SWIGLU_ALPHA * gg) * (uu + 1.0)
        y_ref[...] = _dot(act.astype(BF16), wdn_bf[...]) + bdn_ref[...]

    @pl.when(jnp.logical_not(active))
    def _():
        y_ref[...] = jnp.zeros_like(y_ref)

    @pl.when(g == last)
    def _():
        gather(nxt, 1 - slot, False)


def _moe_grouped(layer, tile_expert, n_tiles, src, h2, w_gu, b_gu, w_dn, b_dn):
    depth, n_exp, d, f2 = w_gu.shape
    f = f2 // 2
    g_max = tile_expert.shape[0]
    tm = MOE_TILE
    gu_rows, dn_rows = 1024, 128
    anyspec = pl.BlockSpec(memory_space=pl.ANY)
    grid_spec = pltpu.PrefetchScalarGridSpec(
        num_scalar_prefetch=3,
        grid=(g_max,),
        in_specs=[anyspec, anyspec, anyspec,
                  pl.BlockSpec((None, None, 1, f2), lambda g, te, nt, sr: (layer, te[g], 0, 0)),
                  pl.BlockSpec((None, None, 1, d), lambda g, te, nt, sr: (layer, te[g], 0, 0))],
        out_specs=pl.BlockSpec((tm, d), lambda g, te, nt, sr: (g, 0)),
        scratch_shapes=[pltpu.VMEM((2, tm, d), F32), pltpu.VMEM((d, f2), BF16), pltpu.VMEM((f, d), BF16),
                        pltpu.VMEM((2, gu_rows, f2), F32), pltpu.VMEM((2, dn_rows, d), F32),
                        pltpu.SemaphoreType.DMA((2,)), pltpu.SemaphoreType.DMA((2,))],
    )
    return pl.pallas_call(
        functools.partial(_moe_kernel, (layer, d, f)),
        grid_spec=grid_spec,
        out_shape=jax.ShapeDtypeStruct((g_max * tm, d), F32),
        compiler_params=_cparams(("arbitrary",)),
        name="moe_grouped",
    )(tile_expert, n_tiles, src, h2, w_gu, w_dn, b_gu.reshape(depth, n_exp, 1, f2), b_dn.reshape(depth, n_exp, 1, d))


def _combine_kernel(n_prompt_tiles, pos_ref, x1_ref, gate_ref, g2s_ref, g2p_ref, y_ref, o_ref, ybuf, sem):
    i = pl.program_id(0)
    last = pl.num_programs(0) - 1
    slot = i % 2
    tm = TOKEN_TILE
    is_sample = i >= n_prompt_tiles

    def gather(tile, sl, start):
        _row_gather(y_ref, pos_ref, tile * (tm * TOP_K), tm * TOP_K,
                    lambda n: ybuf.at[sl, n % TOP_K, pl.ds(n // TOP_K, 1), :], sem.at[sl], start)

    @pl.when(i == 0)
    def _():
        gather(i, slot, True)

    gather(i, slot, False)
    nxt = jnp.minimum(i + 1, last)
    gather(nxt, 1 - slot, True)

    gate = gate_ref[...]
    moe = gate[:, 0:1] * ybuf[slot, 0]
    for k in range(1, TOP_K):
        moe = moe + gate[:, k:k + 1] * ybuf[slot, k]
    o_ref[...] = x1_ref[...] + _pick(is_sample, g2s_ref, g2p_ref) * moe

    @pl.when(i == last)
    def _():
        gather(nxt, 1 - slot, False)


def _combine(pos, x1, gates, y, mod, mod4, ms):
    t, d = x1.shape
    tile = pl.BlockSpec((TOKEN_TILE, d), lambda i, p: (i, 0))
    l, tps, last = ms.layer, ms.tiles_per_seq, ms.n_seq - 1
    grid_spec = pltpu.PrefetchScalarGridSpec(
        num_scalar_prefetch=1,
        grid=(t // TOKEN_TILE,),
        in_specs=[tile, pl.BlockSpec((TOKEN_TILE, TOP_K), lambda i, p: (i, 0)),
                  pl.BlockSpec((None, TOKEN_TILE, d), lambda i, p: (l, 0, 5)),
                  pl.BlockSpec((None, None, 1, d), lambda i, p: (l, TOKEN_TILE + jnp.minimum(i // tps, last), 0, 5)),
                  pl.BlockSpec(memory_space=pl.ANY)],
        out_specs=tile,
        scratch_shapes=[pltpu.VMEM((2, TOP_K, TOKEN_TILE, d), F32), pltpu.SemaphoreType.DMA((2,))],
    )
    return pl.pallas_call(
        functools.partial(_combine_kernel, ms.n_prompt_tiles),
        grid_spec=grid_spec,
        out_shape=jax.ShapeDtypeStruct((t, d), F32),
        compiler_params=_cparams(("arbitrary",)),
        name="moe_combine",
    )(pos.reshape(-1), x1, gates, mod, mod4, y)


def _permute_w_in(w_in, sizes):
    (fq, fk, fv, ff, dq, dk, dv, cq, ckv, kr) = sizes
    o = [0]
    for s in sizes:
        o.append(o[-1] + s)
    seg = lambda k: w_in[..., o[k]:o[k + 1]]
    pad = LANES - kr - ff
    parts = [seg(0), seg(4), seg(7), seg(1), seg(2), seg(5), seg(6), seg(8), seg(9), seg(3),
             jnp.zeros(w_in.shape[:-1] + (pad,), w_in.dtype)]
    off, acc = {}, 0
    for name, p in zip(("fq", "dq", "cq", "fkv", "_fv", "dk", "dv", "ckv", "tail", "_ff", "_pad"), parts):
        off[name] = acc
        acc += p.shape[-1]
    return jnp.concatenate(parts, axis=-1), off


def _rope_tables(positions):
    half = HEAD64 // 2
    inv_freq = ROPE_THETA ** (-jnp.arange(half, dtype=F32) / half)
    ang = positions.astype(F32)[:, None] * inv_freq[None, :]
    cos, sin = jnp.cos(ang), jnp.sin(ang)
    zero = jnp.zeros_like(sin)
    c = jnp.concatenate([cos] * 4, axis=-1)
    sa = jnp.concatenate([zero, sin, zero, sin], axis=-1)
    sb = jnp.concatenate([-sin, zero, -sin, zero], axis=-1)
    return c, sa, sb


def _indicator(n_heads, width):
    rows = jnp.arange(n_heads * width) // width
    return (rows[:, None] == jnp.arange(n_heads)[None, :]).astype(BF16)


def kernel(x_prompt, x_sample, cache_fox, cache_diff, cache_mla, page_table, c_prompt, c_sample, ada_w, ada_b,
           norm1_g, norm2_g, w_in, fox_fb, fox_qn, fox_kn, diff_qn, diff_kn, diff_lam, diff_subln, mla_cqn,
           mla_ckvn, mla_krn, mla_qn, mla_wuq, mla_wuk, mla_wuv, w_out, router_w, router_b, moe_w_gu, moe_b_gu,
           moe_w_dn, moe_b_dn):
    n_seq, seq, d = x_prompt.shape
    nb, dec_seq, _ = x_sample.shape
    depth = ada_w.shape[0]
    n_fox = fox_fb.shape[1]
    n_mla = mla_wuq.shape[2]
    q_rank = mla_wuq.shape[1]
    kv_rank = mla_wuk.shape[1]
    n_diff = (w_out.shape[1] - n_fox * HEAD64 - n_mla * LANES) // LANES
    n_exp = router_w.shape[-1]
    fox_row, diff_row, mla_row = cache_fox.shape[-1], cache_diff.shape[-1], cache_mla.shape[-1]
    n_pages = page_table.shape[1]
    past_len = n_pages * PAGE
    assert nb * dec_seq == TOKEN_TILE and dec_seq == 1 and kv_rank == LANES and cache_fox.shape[2] == PAGE
    assert seq % ATTN_TILE == 0 and n_pages % DECODE_PAGES == 0 and n_fox == n_mla == 2 * n_diff
    tp = n_seq * seq
    t = tp + TOKEN_TILE
    n_prompt_tiles = tp // TOKEN_TILE
    tiles_per_seq = seq // TOKEN_TILE

    sizes = (n_fox * HEAD64, HEAD64, HEAD64, n_fox, n_diff * LANES, LANES, LANES, q_rank, kv_rank, HEAD64)
    w_in_p, off = _permute_w_in(w_in, sizes)
    pos = jnp.concatenate([jnp.tile(jnp.arange(seq), n_seq), jnp.full((TOKEN_TILE,), past_len)])
    tabs = _rope_tables(pos)
    ln = jnp.arange(LANES)
    b64 = (ln[:, None] // HEAD64 == ln[None, :] // HEAD64).astype(BF16)
    tri_incl = (ln[None, :] <= ln[:, None]).astype(BF16)
    tri_strict = (ln[None, :] < ln[:, None]).astype(BF16)
    uext = jnp.concatenate([(ln[:, None] > ln[None, :]).astype(BF16), jnp.ones((LANES, LANES), BF16)], axis=1)
    indn, indr = _indicator(n_mla, MLA_NOPE), _indicator(n_mla, HEAD64)
    consts = (b64, tri_incl, indn, indn.T, indr, indr.T)
    wuq_p = jnp.concatenate([mla_wuq[..., :MLA_NOPE].reshape(depth, q_rank, -1),
                             mla_wuq[..., MLA_NOPE:].reshape(depth, q_rank, -1)], axis=-1)
    wuk_t = jnp.transpose(mla_wuk, (0, 2, 3, 1))
    wuv_h = jnp.transpose(mla_wuv, (0, 2, 1, 3)).astype(BF16)
    rw_hi = router_w.astype(BF16)
    rw_lo = (router_w - rw_hi.astype(F32)).astype(BF16)
    two = lambda g: jnp.concatenate([g, g], axis=-1)
    cache_fox_t = jnp.swapaxes(cache_fox, 2, 3)
    cache_mla_t = jnp.swapaxes(cache_mla, 2, 3)

    c_all =jnp.concatenate([c_sample, c_prompt], axis=0)
    mod = _ada_mod(c_all, ada_w, ada_b)
    mod4 = mod.reshape(depth, mod.shape[1], 1, mod.shape[2])

    x = jnp.concatenate([x_prompt.reshape(tp, d), x_sample.reshape(TOKEN_TILE, d)], axis=0)
    cfg = (n_prompt_tiles, tiles_per_seq, n_fox, n_diff, n_mla, q_rank, off)
    dims = dict(fox_row=fox_row, diff_row=diff_row, mla_row=mla_row)
    g_max = (t * TOP_K) // MOE_TILE + n_exp
    rows_out = []

    for l in range(depth):
        lam_init = 0.8 - 0.6 * math.exp(-0.3 * l)
        ms = _ModSpecs(l, d, n_prompt_tiles, tiles_per_seq, n_seq)
        h1 = _prenorm1(x, norm1_g[l][None], mod, mod4, ms)
        z = _matmul(h1, w_in_p, l, 640 if t % 640 == 0 else TOKEN_TILE, 1152 if w_in_p.shape[-1] % 1152 == 0 else LANES,
                    F32, "in_proj")
        gains = (two(fox_qn[l])[None], jnp.concatenate([fox_kn[l], jnp.ones((HEAD64,), F32)])[None],
                 two(diff_qn[l])[None], two(diff_kn[l])[None], mla_cqn[l][None], mla_ckvn[l][None],
                 two(mla_krn[l])[None], mla_qn[l][None, :MLA_NOPE], two(mla_qn[l][MLA_NOPE:])[None], fox_fb[l][None])
        pp = _prep(z, tabs, consts, gains, wuq_p[l], wuk_t[l], cfg, dims)
        rows_out.append((pp["frow"], pp["drow"], pp["mrow"]))

        nq = seq // ATTN_TILE
        cum_t = jnp.transpose(pp["cum"][:tp].reshape(n_seq, nq, ATTN_TILE, n_fox), (0, 1, 3, 2))
        of, od, om = _prompt_attention(pp, cum_t, diff_lam[l], diff_subln[l][None], wuv_h[l], lam_init,
                                       n_seq, seq, (n_fox, n_diff, n_mla))

        s_ = slice(tp, t)
        qf_s = pp["qf"][s_].reshape(nb, n_fox, HEAD64)
        qd_s = pp["qd"][s_].reshape(nb, n_diff, 2, HEAD64)
        zq = jnp.zeros_like(qd_s[:, :, 0])
        qd_s = jnp.concatenate([jnp.concatenate([qd_s[:, :, 0], zq], axis=-1),
                                jnp.concatenate([zq, qd_s[:, :, 1]], axis=-1)], axis=1)
        qm_s = jnp.concatenate([pp["qlat"][s_].reshape(nb, n_mla, LANES),
                                pp["qrope"][s_].reshape(nb, n_mla, HEAD64)], axis=-1)
        o_f, o_d, o_l = _decode_attention(l, page_table, qf_s, qd_s, qm_s,
                                          pp["frow"][s_][:, None], pp["drow"][s_][:, None], pp["mrow"][s_][:, None],
                                          pp["frow"][s_][:, LANES:, None], uext, cache_fox_t, cache_diff, cache_mla_t)
        o_s = _sample_merge(o_f.reshape(nb, -1), o_d[:, :n_diff].reshape(nb, -1),
                            o_d[:, n_diff:].reshape(nb, -1), o_l.reshape(nb, -1),
                            diff_lam[l], diff_subln[l][None], wuv_h[l], lam_init, d)

        o_all = jnp.concatenate([jnp.concatenate([of, od, om], axis=-1), o_s], axis=0)
        ao = _matmul(o_all, w_out, l, 640 if t % 640 == 0 else TOKEN_TILE, 1024, F32, "out_proj")

        x1, h2, top_i, gates, rank, counts = _prenorm2(x, ao, norm2_g[l][None], mod, mod4, ms,
                                                      rw_hi[l], rw_lo[l], router_b[l][None], tri_strict)

        cnt = counts[0].astype(I32)
        tiles_e = (cnt + MOE_TILE - 1) // MOE_TILE
        tile_end = jnp.cumsum(tiles_e)
        n_tiles = tile_end[-1]
        offs = (tile_end - tiles_e) * MOE_TILE
        pos_rows = offs[top_i] + rank
        gi = jnp.arange(g_max)
        tile_expert = jnp.minimum(jnp.sum((tile_end[None, :] <= jnp.minimum(gi, n_tiles - 1)[:, None]).astype(I32), axis=1),
                                  n_exp - 1)
        src = jnp.zeros((g_max * MOE_TILE,), I32).at[pos_rows.reshape(-1)].set(
            jnp.repeat(jnp.arange(t, dtype=I32), TOP_K))
        y = _moe_grouped(l, tile_expert, n_tiles.reshape(1).astype(I32), src, h2, moe_w_gu, moe_b_gu, moe_w_dn, moe_b_dn)
        x = _combine(pos_rows.astype(I32), x1, gates, y, mod, mod4, ms)

    def stack(k, lo, hi, shape):
        return jnp.stack([r[k][lo:hi].reshape(shape + (r[k].shape[-1],)) for r in rows_out])

    return (x[:tp].reshape(n_seq, seq, d), x[tp:].reshape(nb, dec_seq, d),
            stack(0, 0, tp, (n_seq, seq)), stack(1, 0, tp, (n_seq, seq)), stack(2, 0, tp, (n_seq, seq)),
            stack(0, tp, t, (nb, dec_seq)), stack(1, tp, t, (nb, dec_seq)), stack(2, tp, t, (nb, dec_seq)))
```

```python
import functools
import math

import jax
import jax.numpy as jnp
from jax import lax
from jax.experimental import pallas as pl
from jax.experimental.pallas import tpu as pltpu

F32 = jnp.float32
BF16 = jnp.bfloat16
I32 = jnp.int32

LANES = 128
SUBLANES = 8
VMEM_BYTES_V7X = 64 * 1024 * 1024
VMEM_LIMIT = 56 * 1024 * 1024

HEAD64 = 64
MLA_NOPE = 128
MLA_QK = MLA_NOPE + HEAD64
TOP_K = 4
SWIGLU_ALPHA = 1.702
SWIGLU_LIMIT = 7.0
ROPE_THETA = 10000.0
NORM_EPS = 1e-6
NEG_BIG = -1e30

TOKEN_TILE = 128
PAGE = 128
ATTN_TILE = 256
ATTN_SLABS = 2
MOE_TILE = 256
DECODE_PAGES = 32


def _cparams(sem, vmem=VMEM_LIMIT):
    return pltpu.CompilerParams(dimension_semantics=sem, vmem_limit_bytes=vmem)


def _dot(a, b):
    return jnp.dot(a, b, preferred_element_type=F32)


def _dot_nt(a, b):
    return lax.dot_general(a, b, (((1,), (1,)), ((), ())), preferred_element_type=F32)


def _split2(x):
    hi = x.astype(BF16)
    lo = (x - hi.astype(F32)).astype(BF16)
    return hi, lo


def _split3(x):
    hi = x.astype(BF16)
    r = x - hi.astype(F32)
    mid = r.astype(BF16)
    lo = (r - mid.astype(F32)).astype(BF16)
    return hi, mid, lo


def _dot2(x, m):
    hi, lo = _split2(x)
    return _dot(hi, m) + _dot(lo, m)


def _lane(shape):
    return lax.broadcasted_iota(I32, shape, len(shape) - 1)


def _ada_kernel(c_ref, w_ref, b_ref, o_ref):
    c = c_ref[...]
    a = (c * jax.nn.sigmoid(c)).astype(BF16)
    o_ref[...] = _dot(a, w_ref[...].astype(BF16)) + b_ref[...]


def _ada_mod(c_all, ada_w, ada_b):
    depth, d, n = ada_w.shape
    r = c_all.shape[0]
    tn = 512
    return pl.pallas_call(
        _ada_kernel,
        grid=(depth, n // tn),
        in_specs=[
            pl.BlockSpec((r, d), lambda l, j: (0, 0)),
            pl.BlockSpec((None, d, tn), lambda l, j: (l, 0, j)),
            pl.BlockSpec((None, 1, tn), lambda l, j: (l, 0, j)),
        ],
        out_specs=pl.BlockSpec((None, r, tn), lambda l, j: (l, 0, j)),
        out_shape=jax.ShapeDtypeStruct((depth, r, n), F32),
        compiler_params=_cparams(("arbitrary", "arbitrary")),
        name="ada_mod",
    )(c_all, ada_w, ada_b.reshape(depth, 1, n))


class _ModSpecs:
    def __init__(self, layer, d, n_prompt_tiles, tiles_per_seq, n_seq):
        self.layer, self.d = layer, d
        self.n_prompt_tiles, self.tiles_per_seq, self.n_seq = n_prompt_tiles, tiles_per_seq, n_seq

    def sample(self, piece):
        l = self.layer
        return pl.BlockSpec((None, TOKEN_TILE, self.d), lambda i: (l, 0, piece))

    def prompt(self, piece):
        l, tps, last = self.layer, self.tiles_per_seq, self.n_seq - 1
        return pl.BlockSpec((None, None, 1, self.d),
                            lambda i: (l, TOKEN_TILE + jnp.minimum(i // tps, last), 0, piece))


def _pick(is_sample, sample_ref, prompt_ref):
    return jnp.where(is_sample, sample_ref[...], prompt_ref[...])


def _rms(x):
    return x * lax.rsqrt(jnp.mean(x * x, axis=-1, keepdims=True) + NORM_EPS)


def _prenorm1_kernel(n_prompt_tiles, x_ref, g_ref, shs_ref, shp_ref, scs_ref, scp_ref, h_ref):
    is_sample = pl.program_id(0) >= n_prompt_tiles
    shift = _pick(is_sample, shs_ref, shp_ref)
    scale = _pick(is_sample, scs_ref, scp_ref)
    h_ref[...] = (_rms(x_ref[...]) * g_ref[...] * (1.0 + scale) + shift).astype(h_ref.dtype)


def _prenorm1(x, norm_g, mod, mod4, ms):
    t, d = x.shape
    tile = pl.BlockSpec((TOKEN_TILE, d), lambda i: (i, 0))
    return pl.pallas_call(
        functools.partial(_prenorm1_kernel, ms.n_prompt_tiles),
        grid=(t // TOKEN_TILE,),
        in_specs=[tile, pl.BlockSpec((1, d), lambda i: (0, 0)),
                  ms.sample(0), ms.prompt(0), ms.sample(1), ms.prompt(1)],
        out_specs=tile,
        out_shape=jax.ShapeDtypeStruct((t, d), BF16),
        compiler_params=_cparams(("arbitrary",)),
        name="prenorm1",
    )(x, norm_g, mod, mod4, mod, mod4)


def _mm_kernel(a_ref, w_ref, o_ref, wb_ref):
    @pl.when(pl.program_id(1) == 0)
    def _():
        wb_ref[...] = w_ref[...].astype(BF16)

    o_ref[...] = _dot(a_ref[...], wb_ref[...]).astype(o_ref.dtype)


def _matmul(a, w, layer, tm, tn, out_dtype, name):
    m, k = a.shape
    n = w.shape[-1]
    assert m % tm == 0 and n % tn == 0
    return pl.pallas_call(
        _mm_kernel,
        grid=(n // tn, m // tm),
        in_specs=[pl.BlockSpec((tm, k), lambda j, i: (i, 0)),
                  pl.BlockSpec((None, k, tn), lambda j, i: (layer, 0, j), pipeline_mode=pl.Buffered(1))],
        out_specs=pl.BlockSpec((tm, tn), lambda j, i: (i, j)),
        out_shape=jax.ShapeDtypeStruct((m, n), out_dtype),
        scratch_shapes=[pltpu.VMEM((k, tn), BF16)],
        compiler_params=_cparams(("arbitrary", "arbitrary")),
        name=name,
    )(a, w)


def _rope(x, c, sa, sb):
    return x * c + pltpu.roll(x, 32, 1) * sa + pltpu.roll(x, 96, 1) * sb


def _rms64(x, b64):
    return x * lax.rsqrt(_dot2(x * x, b64) * (1.0 / HEAD64) + NORM_EPS)


def _prep_kernel(cfg, z_ref, c_ref, sa_ref, sb_ref, b64_ref, tri_ref,
                 indn_ref, indnt_ref, indr_ref, indrt_ref,
                 fqn_ref, fkn_ref, dqn_ref, dkn_ref, cqn_ref, ckvn_ref, krn_ref, qnn_ref, qnr_ref, fb_ref,
                 wuq_ref, wuk_ref,
                 qf_ref, qd_ref, qlat_ref, qrope_ref, frow_ref, drow_ref, mrow_ref,
                 fkk_ref, fvv_ref, dk_ref, dv_ref, mk_ref, cum_ref,
                 wuq_bf, wuk_bf, carry_ref):
    (n_prompt_tiles, tiles_per_seq, n_fox, n_diff, n_mla, q_rank, off) = cfg
    i = pl.program_id(0)
    is_sample = i >= n_prompt_tiles
    tm = TOKEN_TILE

    @pl.when(i == 0)
    def _():
        wuq_bf[...] = wuq_ref[...].astype(BF16)
        wuk_bf[...] = wuk_ref[...].astype(BF16)

    c, sa, sb = c_ref[...], sa_ref[...], sb_ref[...]
    b64 = b64_ref[...]
    lane = _lane((tm, LANES))
    left = lane < HEAD64

    for s in range(n_fox // 2):
        x = z_ref[:, off["fq"] + s * LANES: off["fq"] + (s + 1) * LANES]
        qf_ref[:, s * LANES:(s + 1) * LANES] = (_rms64(x, b64) * (fqn_ref[...] * 0.125)).astype(BF16)

    for s in range(n_diff):
        x = z_ref[:, off["dq"] + s * LANES: off["dq"] + (s + 1) * LANES]
        y = _rope(_rms64(x, b64) * dqn_ref[...], c, sa, sb)
        qd_ref[:, s * LANES:(s + 1) * LANES] = (y * 0.125).astype(BF16)

    kv = z_ref[:, off["fkv"]: off["fkv"] + LANES]
    kvn = jnp.where(left, _rms64(kv, b64) * fkn_ref[...], kv)
    kv_sw = pltpu.roll(kvn, HEAD64, 1)
    fkk_ref[...] = jnp.where(left, kvn, kv_sw).astype(BF16)
    fvv_ref[...] = jnp.where(left, kv_sw, kvn).astype(BF16)
    tail = z_ref[:, off["tail"]: off["tail"] + LANES]
    ff = tail[:, HEAD64:HEAD64 + n_fox] + fb_ref[...]
    logf = jnp.minimum(ff, 0.0) - jnp.log(1.0 + jnp.exp(-jnp.abs(ff)))
    frow_ref[:, 0:LANES] = kvn
    frow_ref[:, LANES:LANES + n_fox] = logf

    @pl.when(i % tiles_per_seq == 0)
    def _():
        carry_ref[...] = jnp.zeros_like(carry_ref)

    hi, mid, lo = _split3(logf)
    tri = tri_ref[...]
    cum = carry_ref[...] + _dot(tri, hi) + _dot(tri, mid) + _dot(tri, lo)
    carry_ref[...] = cum[tm - 1:tm, :]
    cum_ref[...] = jnp.where(is_sample, logf, cum)

    dk = _rope(_rms64(z_ref[:, off["dk"]: off["dk"] + LANES], b64) * dkn_ref[...], c, sa, sb)
    dv = z_ref[:, off["dv"]: off["dv"] + LANES]
    drow_ref[:, 0:LANES] = dk
    drow_ref[:, LANES:2 * LANES] = dv
    dk_ref[...] = dk.astype(BF16)
    dv_ref[...] = dv.astype(BF16)

    ckv = _rms(z_ref[:, off["ckv"]: off["ckv"] + LANES]) * ckvn_ref[...]
    kr = _rope(_rms64(tail, b64) * krn_ref[...], c, sa, sb)
    mrow_ref[:, 0:LANES] = ckv
    mrow_ref[:, LANES:LANES + HEAD64] = kr[:, 0:HEAD64]
    mk_ref[:, 0:LANES] = ckv.astype(BF16)
    mk_ref[:, LANES:2 * LANES] = jnp.where(left, kr, pltpu.roll(kr, HEAD64, 1)).astype(BF16)

    cq = (_rms(z_ref[:, off["cq"]: off["cq"] + q_rank]) * cqn_ref[...]).astype(BF16)
    q = _dot(cq, wuq_bf[...])
    nn = n_mla * MLA_NOPE
    q_n, q_r = q[:, :nn], q[:, nn:]
    ss = _dot2(q_n * q_n, indn_ref[...]) + _dot2(q_r * q_r, indr_ref[...])
    r = lax.rsqrt(ss * (1.0 / MLA_QK) + NORM_EPS)
    mla_scale = MLA_QK ** -0.5
    r_n = _dot2(r, indnt_ref[...])
    for h in range(n_mla):
        qh = (q_n[:, h * LANES:(h + 1) * LANES] * r_n[:, h * LANES:(h + 1) * LANES] * qnn_ref[...]).astype(BF16)
        qlat_ref[:, h * LANES:(h + 1) * LANES] = (_dot(qh, wuk_bf[h]) * mla_scale).astype(BF16)
    r_r = _dot2(r, indrt_ref[...])
    for s in range(n_mla // 2):
        sl = slice(s * LANES, (s + 1) * LANES)
        y = _rope(q_r[:, sl] * r_r[:, sl] * qnr_ref[...], c, sa, sb)
        qrope_ref[:, sl] = (y * mla_scale).astype(BF16)


def _prep(z, tabs, consts, gains, wuq_p, wuk_t, cfg, dims):
    t, zc = z.shape
    (n_prompt_tiles, tiles_per_seq, n_fox, n_diff, n_mla, q_rank, off) = cfg
    tm = TOKEN_TILE
    row = lambda w: pl.BlockSpec((tm, w), lambda i: (i, 0))
    full = lambda a: pl.BlockSpec(a.shape, lambda i: (0,) * a.ndim)
    in_arrays = [z, *tabs, *consts, *gains, wuq_p, wuk_t]
    in_specs = [row(zc), row(LANES), row(LANES), row(LANES)] + [full(a) for a in in_arrays[4:]]
    widths = [("qf", n_fox * HEAD64, BF16), ("qd", n_diff * LANES, BF16), ("qlat", n_mla * LANES, BF16),
              ("qrope", n_mla * HEAD64, BF16), ("frow", dims["fox_row"], F32), ("drow", dims["diff_row"], F32),
              ("mrow", dims["mla_row"], F32), ("fkk", LANES, BF16), ("fvv", LANES, BF16), ("dk", LANES, BF16),
              ("dv", LANES, BF16), ("mk", 2 * LANES, BF16), ("cum", n_fox, F32)]
    outs = pl.pallas_call(
        functools.partial(_prep_kernel, cfg),
        grid=(t // tm,),
        in_specs=in_specs,
        out_specs=[row(w) for _, w, _ in widths],
        out_shape=[jax.ShapeDtypeStruct((t, w), dt) for _, w, dt in widths],
        scratch_shapes=[pltpu.VMEM(wuq_p.shape, BF16), pltpu.VMEM(wuk_t.shape, BF16), pltpu.VMEM((1, n_fox), F32)],
        compiler_params=_cparams(("arbitrary",)),
        name="qkv_prep",
    )(*in_arrays)
    return {k: v for (k, _, _), v in zip(widths, outs)}


def _online_update(s, m, l):
    m_new = jnp.maximum(m, jnp.max(s, axis=-1, keepdims=True))
    p = jnp.exp(s - m_new)
    alpha = jnp.exp(m - m_new)
    return m_new, alpha * l + jnp.sum(p, axis=-1, keepdims=True), alpha, p


def _causal_mask(tq):
    r = lax.broadcasted_iota(I32, (tq, tq), 0)
    c = lax.broadcasted_iota(I32, (tq, tq), 1)
    return c <= r


def _fox_attn_kernel(n_heads, q_ref, kk_ref, vv_ref, ckt_ref, o_ref):
    tq = ATTN_TILE
    i = pl.program_id(1)
    lane = _lane((tq, LANES))
    left = lane < HEAD64
    mask = _causal_mask(tq)
    one = (jnp.full((tq, 1), NEG_BIG, F32), jnp.zeros((tq, 1), F32), jnp.zeros((tq, LANES), F32))
    for s0 in range(0, n_heads // 2, ATTN_SLABS):
        chains = []
        for s in range(s0, s0 + ATTN_SLABS):
            qs = q_ref[:, s * LANES:(s + 1) * LANES]
            chains.append((jnp.where(left, qs, jnp.zeros_like(qs)), 2 * s))
            chains.append((jnp.where(left, jnp.zeros_like(qs), qs), 2 * s + 1))

        def step(j, carry, masked, chains=chains):
            rows = pl.ds(pl.multiple_of(j * tq, tq), tq)
            k = kk_ref[rows, :]
            v = vv_ref[rows, :]
            new = []
            for (q, h), (m, l, acc) in zip(chains, carry):
                sc = _dot_nt(q, k) - ckt_ref[j, h:h + 1, :]
                if masked:
                    sc = jnp.where(mask, sc, NEG_BIG)
                m, l, alpha, p = _online_update(sc, m, l)
                new.append((m, l, alpha * acc + _dot(p.astype(BF16), v)))
            return tuple(new)

        carry = lax.fori_loop(0, i, lambda j, c: step(j, c, False), (one,) * len(chains))
        res = step(i, carry, True)
        for n, s in enumerate(range(s0, s0 + ATTN_SLABS)):
            (_, l0, a0), (_, l1, a1) = res[2 * n], res[2 * n + 1]
            o_ref[:, s * LANES:(s + 1) * LANES] = jnp.where(left, a0 / l0, a1 / l1).astype(o_ref.dtype)


def _diff_lambda(lam_ref, lam_init):
    lv = lam_ref[...]
    a = jnp.sum(jnp.sum(lv[0:1] * lv[1:2], axis=1, keepdims=True), axis=0, keepdims=True)
    b = jnp.sum(jnp.sum(lv[2:3] * lv[3:4], axis=1, keepdims=True), axis=0, keepdims=True)
    return jnp.exp(a) - jnp.exp(b) + lam_init


def _diff_attn_kernel(n_heads, lam_init, q_ref, k_ref, v_ref, lam_ref, sub_ref, o_ref):
    tq = ATTN_TILE
    i = pl.program_id(1)
    lane = _lane((tq, LANES))
    left = lane < HEAD64
    mask = _causal_mask(tq)
    lam = _diff_lambda(lam_ref, lam_init)
    one = (jnp.full((tq, 1), NEG_BIG, F32), jnp.zeros((tq, 1), F32), jnp.zeros((tq, LANES), F32))
    for h0 in range(0, n_heads, ATTN_SLABS):
        qs_list = []
        for h in range(h0, h0 + ATTN_SLABS):
            qs = q_ref[:, h * LANES:(h + 1) * LANES]
            qs_list += [jnp.where(left, qs, jnp.zeros_like(qs)), jnp.where(left, jnp.zeros_like(qs), qs)]

        def step(j, carry, masked, qs_list=qs_list):
            rows = pl.ds(pl.multiple_of(j * tq, tq), tq)
            k = k_ref[rows, :]
            v = v_ref[rows, :]
            new = []
            for q, (m, l, acc) in zip(qs_list, carry):
                sc = _dot_nt(q, k)
                if masked:
                    sc = jnp.where(mask, sc, NEG_BIG)
                m, l, alpha, p = _online_update(sc, m, l)
                new.append((m, l, alpha * acc + _dot(p.astype(BF16), v)))
            return tuple(new)

        carry = lax.fori_loop(0, i, lambda j, c: step(j, c, False), (one,) * len(qs_list))
        res = step(i, carry, True)
        for n, h in enumerate(range(h0, h0 + ATTN_SLABS)):
            (_, l1, a1), (_, l2, a2) = res[2 * n], res[2 * n + 1]
            od = a1 / l1 - lam * (a2 / l2)
            o_ref[:, h * LANES:(h + 1) * LANES] = (_rms(od) * sub_ref[...] * (1.0 - lam_init)).astype(o_ref.dtype)


def _mla_attn_kernel(n_heads, qlat_ref, qrope_ref, k_ref, wuv_ref, o_ref):
    tq = ATTN_TILE
    i = pl.program_id(1)
    lane = _lane((tq, LANES))
    left = lane < HEAD64
    mask = _causal_mask(tq)
    one = (jnp.full((tq, 1), NEG_BIG, F32), jnp.zeros((tq, 1), F32), jnp.zeros((tq, LANES), F32))
    for s0 in range(0, n_heads // 2, ATTN_SLABS):
        qs_list = []
        for s in range(s0, s0 + ATTN_SLABS):
            qr = qrope_ref[:, s * LANES:(s + 1) * LANES]
            for half in range(2):
                qs_list.append(jnp.concatenate(
                    [qlat_ref[:, (2 * s + half) * LANES:(2 * s + half + 1) * LANES],
                     jnp.where(left if half == 0 else jnp.logical_not(left), qr, jnp.zeros_like(qr))], axis=-1))

        def step(j, carry, masked, qs_list=qs_list):
            k = k_ref[pl.ds(pl.multiple_of(j * tq, tq), tq), :]
            new = []
            for q, (m, l, acc) in zip(qs_list, carry):
                sc = _dot_nt(q, k)
                if masked:
                    sc = jnp.where(mask, sc, NEG_BIG)
                m, l, alpha, p = _online_update(sc, m, l)
                new.append((m, l, alpha * acc + _dot(p.astype(BF16), k[:, 0:LANES])))
            return tuple(new)

        carry = lax.fori_loop(0, i, lambda j, c: step(j, c, False), (one,) * len(qs_list))
        for n, (m, l, acc) in enumerate(step(i, carry, True)):
            h = 2 * s0 + n
            o_ref[:, h * LANES:(h + 1) * LANES] = _dot((acc / l).astype(BF16), wuv_ref[h]).astype(o_ref.dtype)


def _prompt_attention(pp, cum_t, diff_lam, diff_subln, wuv_h, lam_init, n_seq, seq, heads):
    n_fox, n_diff, n_mla = heads
    tq = ATTN_TILE
    nq = seq // tq
    tp = n_seq * seq
    qrow = lambda w: pl.BlockSpec((tq, w), lambda b, i: (b * nq + i, 0))
    seqrows = lambda w: pl.BlockSpec((seq, w), lambda b, i: (b, 0))
    full = lambda a: pl.BlockSpec(a.shape, lambda b, i: (0,) * a.ndim)
    cp = _cparams(("arbitrary", "arbitrary"))
    of = pl.pallas_call(
        functools.partial(_fox_attn_kernel, n_fox),
        grid=(n_seq, nq),
        in_specs=[qrow(n_fox * HEAD64), seqrows(LANES), seqrows(LANES),
                  pl.BlockSpec((None, nq, n_fox, tq), lambda b, i: (b, 0, 0, 0))],
        out_specs=qrow(n_fox * HEAD64),
        out_shape=jax.ShapeDtypeStruct((tp, n_fox * HEAD64), BF16),
        compiler_params=cp, name="fox_attn",
    )(pp["qf"], pp["fkk"], pp["fvv"], cum_t)
    od = pl.pallas_call(
        functools.partial(_diff_attn_kernel, n_diff, lam_init),
        grid=(n_seq, nq),
        in_specs=[qrow(n_diff * LANES), seqrows(LANES), seqrows(LANES), full(diff_lam), full(diff_subln)],
        out_specs=qrow(n_diff * LANES),
        out_shape=jax.ShapeDtypeStruct((tp, n_diff * LANES), BF16),
        compiler_params=cp, name="diff_attn",
    )(pp["qd"], pp["dk"], pp["dv"], diff_lam, diff_subln)
    om = pl.pallas_call(
        functools.partial(_mla_attn_kernel, n_mla),
        grid=(n_seq, nq),
        in_specs=[qrow(n_mla * LANES), qrow(n_mla * HEAD64), seqrows(2 * LANES), full(wuv_h)],
        out_specs=qrow(n_mla * LANES),
        out_shape=jax.ShapeDtypeStruct((tp, n_mla * LANES), BF16),
        compiler_params=cp, name="mla_attn",
    )(pp["qlat"], pp["qrope"], pp["mk"], wuv_h)
    return of, od, om


def _page_copy(cache_ref, buf_ref, sem_ref, layer, page, slot, g):
    return pltpu.make_async_copy(cache_ref.at[layer, page], buf_ref.at[slot, g], sem_ref.at[slot])


def _decode_kernel(cfg, pt_ref, qf_ref, qd_ref, qm_ref, fnew_ref, dnew_ref, mnew_ref, lfnew_ref, uext_ref,
                   cf_ref, cd_ref, cm_ref,
                   of_ref, od_ref, ol_ref,
                   fbuf, dbuf, mbuf, sem_f, sem_d, sem_m, st_ref, accf_ref, accd_ref, accm_ref, carry_ref):
    (layer, n_pages, n_chunks, n_heads) = cfg
    gpp = DECODE_PAGES
    t = pl.program_id(0)
    n_steps = pl.num_programs(0)
    b = t // n_chunks
    c = t % n_chunks
    slot = t % 2

    def issue(step, sl, start):
        sb = step // n_chunks
        sc = step % n_chunks
        base = sb * n_pages + (n_chunks - 1 - sc) * gpp
        for g in range(gpp):
            page = pt_ref[base + g]
            for cache, buf, sem in ((cf_ref, fbuf, sem_f), (cd_ref, dbuf, sem_d), (cm_ref, mbuf, sem_m)):
                cp = _page_copy(cache, buf, sem, layer, page, sl, g)
                if start:
                    cp.start()
                else:
                    cp.wait()

    @pl.when(t == 0)
    def _():
        issue(t, slot, True)


    qf = qf_ref[...]
    qd = qd_ref[...]
    qm = qm_ref[...]
    hh = n_heads
    accs = (accf_ref, accd_ref, accm_ref)

    @pl.when(c == 0)
    def _():
        rnd = lambda v: v.astype(BF16).astype(F32)
        fnew, dnew, mnew = fnew_ref[...], dnew_ref[...], mnew_ref[...]
        s_f = jnp.sum(qf.astype(F32) * rnd(fnew[:, 0:HEAD64]), axis=-1, keepdims=True)
        s_d = jnp.sum(qd.astype(F32) * rnd(dnew[:, 0:LANES]), axis=-1, keepdims=True)
        s_m = jnp.sum(qm.astype(F32) * rnd(mnew), axis=-1, keepdims=True)
        values = (fnew[:, HEAD64:LANES], dnew[:, LANES:2 * LANES], mnew[:, 0:LANES])
        for k, (sv, vrow) in enumerate(zip((s_f, s_d, s_m), values)):
            st_ref[2 * k + 0] = sv * jnp.ones((hh, LANES), F32)
            st_ref[2 * k + 1] = jnp.ones((hh, LANES), F32)
            accs[k][...] = rnd(vrow) * jnp.ones((hh, 1), F32)
        carry_ref[...] = jnp.zeros_like(carry_ref)

    issue(t, slot, False)
    nxt = jnp.minimum(t + 1, n_steps - 1)
    issue(nxt, 1 - slot, True)

    def update(k, sc, pv_fn):
        m, l = st_ref[2 * k + 0][:, 0:1], st_ref[2 * k + 1][:, 0:1]
        m_new = jnp.maximum(m, jnp.max(sc, axis=-1, keepdims=True))
        p = jnp.exp(sc - m_new).astype(BF16)
        alpha = jnp.exp(m - m_new)
        st_ref[2 * k + 0] = m_new * jnp.ones((hh, LANES), F32)
        st_ref[2 * k + 1] = (alpha * l + jnp.sum(p.astype(F32), axis=-1, keepdims=True)) * jnp.ones((hh, LANES), F32)
        pv = pv_fn(0, p[:, 0:PAGE])
        for g in range(1, gpp):
            pv = pv + pv_fn(g, p[:, g * PAGE:(g + 1) * PAGE])
        accs[k][...] = alpha * accs[k][...] + pv

    parts = []
    for g in range(gpp):
        parts.extend(_split3(fbuf[slot, g, LANES:LANES + hh, :]))
    sfx = _dot(jnp.concatenate(parts, axis=0), uext_ref[...])
    decay = [None] * gpp
    run = carry_ref[...]
    lfnew = lfnew_ref[...]
    for g in reversed(range(gpp)):
        r0 = 3 * hh * g
        sg = sfx[r0:r0 + hh] + sfx[r0 + hh:r0 + 2 * hh] + sfx[r0 + 2 * hh:r0 + 3 * hh]
        decay[g] = (run + lfnew) + sg[:, 0:LANES]
        run = run + sg[:, LANES:2 * LANES]
    carry_ref[...] = run
    s_f = jnp.concatenate([_dot(qf, fbuf[slot, g, 0:HEAD64, :].astype(BF16)) + decay[g] for g in range(gpp)], axis=1)
    update(0, s_f, lambda g, p: _dot_nt(p, fbuf[slot, g, HEAD64:LANES, :].astype(BF16)))

    s_d = jnp.concatenate([_dot_nt(qd, dbuf[slot, g, :, 0:LANES].astype(BF16)) for g in range(gpp)], axis=1)
    update(1, s_d, lambda g, p: _dot(p, dbuf[slot, g, :, LANES:2 * LANES].astype(BF16)))

    s_m = jnp.concatenate([_dot(qm, mbuf[slot, g].astype(BF16)) for g in range(gpp)], axis=1)
    update(2, s_m, lambda g, p: _dot_nt(p, mbuf[slot, g, 0:LANES, :].astype(BF16)))

    @pl.when(c == n_chunks - 1)
    def _():
        of_ref[...] = accf_ref[...] / st_ref[1][:, 0:HEAD64]
        od_ref[...] = accd_ref[...] / st_ref[3]
        ol_ref[...] = accm_ref[...] / st_ref[5]

    @pl.when(t == n_steps - 1)
    def _():
        issue(nxt, 1 - slot, False)


def _decode_attention(layer, page_table, qf, qd, qm, fnew, dnew, mnew, lfnew, uext, cache_fox_t, cache_diff, cache_mla_t):
    nb, n_pages = page_table.shape
    hh = qf.shape[1]
    gpp = DECODE_PAGES
    n_chunks = n_pages // gpp
    fw, dw, mw = cache_fox_t.shape[2], cache_diff.shape[3], cache_mla_t.shape[2]
    cfg = (layer, n_pages, n_chunks, hh)
    per_b = lambda a: pl.BlockSpec((None,) + a.shape[1:], lambda t, pt: (t // n_chunks,) + (0,) * (a.ndim - 1))
    anyspec = pl.BlockSpec(memory_space=pl.ANY)
    out = lambda w: jax.ShapeDtypeStruct((nb, hh, w), F32)
    out_spec = lambda w: pl.BlockSpec((None, hh, w), lambda t, pt: (t // n_chunks, 0, 0))
    grid_spec = pltpu.PrefetchScalarGridSpec(
        num_scalar_prefetch=1,
        grid=(nb * n_chunks,),
        in_specs=[per_b(qf), per_b(qd), per_b(qm), per_b(fnew), per_b(dnew), per_b(mnew), per_b(lfnew),
                  pl.BlockSpec(uext.shape, lambda t, pt: (0, 0)), anyspec, anyspec, anyspec],
        out_specs=[out_spec(HEAD64), out_spec(LANES), out_spec(LANES)],
        scratch_shapes=[pltpu.VMEM((2, gpp, fw, PAGE), F32), pltpu.VMEM((2, gpp, PAGE, dw), F32),
                        pltpu.VMEM((2, gpp, mw, PAGE), F32),
                        pltpu.SemaphoreType.DMA((2,)), pltpu.SemaphoreType.DMA((2,)), pltpu.SemaphoreType.DMA((2,)),
                        pltpu.VMEM((6, hh, LANES), F32), pltpu.VMEM((hh, HEAD64), F32), pltpu.VMEM((hh, LANES), F32),
                        pltpu.VMEM((hh, LANES), F32), pltpu.VMEM((hh, LANES), F32)],
    )
    return pl.pallas_call(
        functools.partial(_decode_kernel, cfg),
        grid_spec=grid_spec,
        out_shape=[out(HEAD64), out(LANES), out(LANES)],
        compiler_params=_cparams(("arbitrary",)),
        name="decode_attn",
    )(page_table.reshape(-1), qf, qd, qm, fnew, dnew, mnew, lfnew, uext, cache_fox_t, cache_diff, cache_mla_t)


def _sample_merge_kernel(n_diff, n_mla, lam_init, of_ref, o1_ref, o2_ref, ol_ref, lam_ref, sub_ref, wuv_ref, o_ref):
    lam = _diff_lambda(lam_ref, lam_init)
    nf = of_ref.shape[1]
    o_ref[:, 0:nf] = of_ref[...].astype(o_ref.dtype)
    for h in range(n_diff):
        sl = slice(h * LANES, (h + 1) * LANES)
        od = o1_ref[:, sl] - lam * o2_ref[:, sl]
        o_ref[:, nf + h * LANES: nf + (h + 1) * LANES] = (_rms(od) * sub_ref[...] * (1.0 - lam_init)).astype(o_ref.dtype)
    base = nf + n_diff * LANES
    for h in range(n_mla):
        sl = slice(h * LANES, (h + 1) * LANES)
        o_ref[:, base + h * LANES: base + (h + 1) * LANES] = _dot(ol_ref[:, sl].astype(BF16), wuv_ref[h]).astype(o_ref.dtype)


def _sample_merge(of, o1, o2, ol, diff_lam, diff_subln, wuv_h, lam_init, d):
    nb = of.shape[0]
    n_diff = o1.shape[1] // LANES
    n_mla = ol.shape[1] // LANES
    args = (of, o1, o2, ol, diff_lam, diff_subln, wuv_h)
    return pl.pallas_call(
        functools.partial(_sample_merge_kernel, n_diff, n_mla, lam_init),
        grid=(1,),
        in_specs=[pl.BlockSpec(a.shape, lambda i, nd=a.ndim: (0,) * nd) for a in args],
        out_specs=pl.BlockSpec((nb, d), lambda i: (0, 0)),
        out_shape=jax.ShapeDtypeStruct((nb, d), BF16),
        compiler_params=_cparams(("arbitrary",)),
        name="sample_merge",
    )(*args)


def _prenorm2_kernel(n_prompt_tiles, n_exp, x_ref, ao_ref, g_ref, g1s_ref, g1p_ref, shs_ref, shp_ref, scs_ref, scp_ref,
                     rw_hi_ref, rw_lo_ref, rb_ref, tri_ref,
                     x1_ref, h_ref, ti_ref, gate_ref, rank_ref, cnt_ref, base_ref):
    i = pl.program_id(0)
    is_sample = i >= n_prompt_tiles
    tm = TOKEN_TILE

    @pl.when(i == 0)
    def _():
        base_ref[...] = jnp.zeros_like(base_ref)

    x1 = x_ref[...] + _pick(is_sample, g1s_ref, g1p_ref) * ao_ref[...]
    x1_ref[...] = x1
    h = _rms(x1) * g_ref[...] * (1.0 + _pick(is_sample, scs_ref, scp_ref)) + _pick(is_sample, shs_ref, shp_ref)
    h_ref[...] = h

    h_hi, h_lo = _split2(h)
    logits = _dot(h_hi, rw_hi_ref[...]) + _dot(h_lo, rw_hi_ref[...]) + _dot(h_hi, rw_lo_ref[...]) + rb_ref[...]
    lane = _lane((tm, n_exp))
    lane_k = _lane((tm, TOP_K))
    work = logits
    vals, sels = [], []
    top_i = jnp.zeros((tm, TOP_K), I32)
    for k in range(TOP_K):
        v = jnp.max(work, axis=-1, keepdims=True)
        idx = jnp.min(jnp.where(work == v, lane, n_exp), axis=-1, keepdims=True)
        sel = lane == idx
        vals.append(v)
        sels.append(sel)
        top_i = jnp.where(lane_k == k, idx, top_i)
        work = jnp.where(sel, -jnp.inf, work)
    es = [jnp.exp(v - vals[0]) for v in vals]
    tot = es[0] + es[1] + es[2] + es[3]
    gates = jnp.zeros((tm, TOP_K), F32)
    for k in range(TOP_K):
        gates = jnp.where(lane_k == k, es[k] / tot, gates)

    onehot = jnp.zeros((tm, n_exp), F32)
    for sel in sels:
        onehot = onehot + sel.astype(F32)
    before = _dot(tri_ref[...], onehot.astype(BF16)) + base_ref[...]
    rank = jnp.zeros((tm, TOP_K), F32)
    for k in range(TOP_K):
        rk = jnp.sum(jnp.where(sels[k], before, 0.0), axis=-1, keepdims=True)
        rank = jnp.where(lane_k == k, rk, rank)
    base_ref[...] = base_ref[...] + jnp.sum(onehot, axis=0, keepdims=True)
    ti_ref[...] = top_i
    gate_ref[...] = gates
    rank_ref[...] = rank.astype(I32)
    cnt_ref[...] = base_ref[...]


def _prenorm2(x, ao, norm_g, mod, mod4, ms, rw_hi, rw_lo, rb, tri_strict):
    t, d = x.shape
    n_exp = rw_hi.shape[1]
    tile = pl.BlockSpec((TOKEN_TILE, d), lambda i: (i, 0))
    small = pl.BlockSpec((TOKEN_TILE, TOP_K), lambda i: (i, 0))
    full = lambda a: pl.BlockSpec(a.shape, lambda i: (0,) * a.ndim)
    return pl.pallas_call(
        functools.partial(_prenorm2_kernel, ms.n_prompt_tiles, n_exp),
        grid=(t // TOKEN_TILE,),
        in_specs=[tile, tile, full(norm_g), ms.sample(2), ms.prompt(2), ms.sample(3), ms.prompt(3),
                  ms.sample(4), ms.prompt(4), full(rw_hi), full(rw_lo), full(rb), full(tri_strict)],
        out_specs=[tile, tile, small, small, small, pl.BlockSpec((1, n_exp), lambda i: (0, 0))],
        out_shape=[jax.ShapeDtypeStruct((t, d), F32), jax.ShapeDtypeStruct((t, d), F32),
                   jax.ShapeDtypeStruct((t, TOP_K), I32), jax.ShapeDtypeStruct((t, TOP_K), F32),
                   jax.ShapeDtypeStruct((t, TOP_K), I32), jax.ShapeDtypeStruct((1, n_exp), F32)],
        scratch_shapes=[pltpu.VMEM((1, n_exp), F32)],
        compiler_params=_cparams(("arbitrary",)),
        name="prenorm2_router",
    )(x, ao, norm_g, mod, mod4, mod, mod4, mod, mod4, rw_hi, rw_lo, rb, tri_strict)


def _row_gather(src_ref, idx_ref, base, n_rows, dst_of, sem, start):
    for n in range(n_rows):
        row = idx_ref[base + n] if start else 0
        cp = pltpu.make_async_copy(src_ref.at[pl.ds(row, 1), :], dst_of(n), sem)
        if start:
            cp.start()
        else:
            cp.wait()


def _moe_kernel(cfg, te_ref, nt_ref, src_ref, h_ref, wgu_ref, wdn_ref, bgu_ref, bdn_ref, y_ref,
                xbuf, wgu_bf, wdn_bf, stage_gu, stage_dn, sem_x, sem_w):
    (layer, d, f) = cfg
    tm = MOE_TILE
    g = pl.program_id(0)
    last = pl.num_programs(0) - 1
    slot = g % 2
    e = te_ref[g]
    active = g < nt_ref[0]
    prev = te_ref[jnp.maximum(g - 1, 0)]
    changed = jnp.logical_or(g == 0, e != prev)

    def gather(tile, sl, start):
        _row_gather(h_ref, src_ref, tile * tm, tm, lambda r: xbuf.at[sl, pl.ds(r, 1), :], sem_x.at[sl], start)

    @pl.when(g == 0)
    def _():
        gather(g, slot, True)

    gather(g, slot, False)
    nxt = jnp.minimum(g + 1, last)
    gather(nxt, 1 - slot, True)

    gu_rows, dn_rows = stage_gu.shape[1], stage_dn.shape[1]
    chunks = [(wgu_ref, stage_gu, wgu_bf, k * gu_rows, gu_rows) for k in range(d // gu_rows)]
    chunks += [(wdn_ref, stage_dn, wdn_bf, k * dn_rows, dn_rows) for k in range(f // dn_rows)]

    @pl.when(jnp.logical_and(active, changed))
    def _():
        copies = [pltpu.make_async_copy(w.at[layer, e, pl.ds(r0, nr), :], st.at[k % 2], sem_w.at[k % 2])
                  for k, (w, st, _, r0, nr) in enumerate(chunks)]
        copies[0].start()
        for k, (_, st, dst, r0, nr) in enumerate(chunks):
            if k + 1 < len(chunks):
                copies[k + 1].start()
            copies[k].wait()
            dst[r0:r0 + nr, :] = st[k % 2].astype(BF16)

    @pl.when(active)
    def _():
        x = xbuf[slot].astype(BF16)
        gu = _dot(x, wgu_bf[...]) + bgu_ref[...]
        gg = jnp.minimum(gu[:, :f], SWIGLU_LIMIT)
        uu = jnp.clip(gu[:, f:], -SWIGLU_LIMIT, SWIGLU_LIMIT)
        act = gg * jax.nn.sigmoid(SWIGLU_ALPHA * gg) * (uu + 1.0)
        y_ref[...] = _dot(act.astype(BF16), wdn_bf[...]) + bdn_ref[...]

    @pl.when(jnp.logical_not(active))
    def _():
        y_ref[...] = jnp.zeros_like(y_ref)

    @pl.when(g == last)
    def _():
        gather(nxt, 1 - slot, False)


def _moe_grouped(layer, tile_expert, n_tiles, src, h2, w_gu, b_gu, w_dn, b_dn):
    depth, n_exp, d, f2 = w_gu.shape
    f = f2 // 2
    g_max = tile_expert.shape[0]
    tm = MOE_TILE
    gu_rows, dn_rows = 1024, 128
    anyspec = pl.BlockSpec(memory_space=pl.ANY)
    grid_spec = pltpu.PrefetchScalarGridSpec(
        num_scalar_prefetch=3,
        grid=(g_max,),
        in_specs=[anyspec, anyspec, anyspec,
                  pl.BlockSpec((None, None, 1, f2), lambda g, te, nt, sr: (layer, te[g], 0, 0)),
                  pl.BlockSpec((None, None, 1, d), lambda g, te, nt, sr: (layer, te[g], 0, 0))],
        out_specs=pl.BlockSpec((tm, d), lambda g, te, nt, sr: (g, 0)),
        scratch_shapes=[pltpu.VMEM((2, tm, d), F32), pltpu.VMEM((d, f2), BF16), pltpu.VMEM((f, d), BF16),
                        pltpu.VMEM((2, gu_rows, f2), F32), pltpu.VMEM((2, dn_rows, d), F32),
                        pltpu.SemaphoreType.DMA((2,)), pltpu.SemaphoreType.DMA((2,))],
    )
    return pl.pallas_call(
        functools.partial(_moe_kernel, (layer, d, f)),
        grid_spec=grid_spec,
        out_shape=jax.ShapeDtypeStruct((g_max * tm, d), F32),
        compiler_params=_cparams(("arbitrary",)),
        name="moe_grouped",
    )(tile_expert, n_tiles, src, h2, w_gu, w_dn, b_gu.reshape(depth, n_exp, 1, f2), b_dn.reshape(depth, n_exp, 1, d))


def _combine_kernel(n_prompt_tiles, pos_ref, x1_ref, gate_ref, g2s_ref, g2p_ref, y_ref, o_ref, ybuf, sem):
    i = pl.program_id(0)
    last = pl.num_programs(0) - 1
    slot = i % 2
    tm = TOKEN_TILE
    is_sample = i >= n_prompt_tiles

    def gather(tile, sl, start):
        _row_gather(y_ref, pos_ref, tile * (tm * TOP_K), tm * TOP_K,
                    lambda n: ybuf.at[sl, n % TOP_K, pl.ds(n // TOP_K, 1), :], sem.at[sl], start)

    @pl.when(i == 0)
    def _():
        gather(i, slot, True)

    gather(i, slot, False)
    nxt = jnp.minimum(i + 1, last)
    gather(nxt, 1 - slot, True)

    gate = gate_ref[...]
    moe = gate[:, 0:1] * ybuf[slot, 0]
    for k in range(1, TOP_K):
        moe = moe + gate[:, k:k + 1] * ybuf[slot, k]
    o_ref[...] = x1_ref[...] + _pick(is_sample, g2s_ref, g2p_ref) * moe

    @pl.when(i == last)
    def _():
        gather(nxt, 1 - slot, False)


def _combine(pos, x1, gates, y, mod, mod4, ms):
    t, d = x1.shape
    tile = pl.BlockSpec((TOKEN_TILE, d), lambda i, p: (i, 0))
    l, tps, last = ms.layer, ms.tiles_per_seq, ms.n_seq - 1
    grid_spec = pltpu.PrefetchScalarGridSpec(
        num_scalar_prefetch=1,
        grid=(t // TOKEN_TILE,),
        in_specs=[tile, pl.BlockSpec((TOKEN_TILE, TOP_K), lambda i, p: (i, 0)),
                  pl.BlockSpec((None, TOKEN_TILE, d), lambda i, p: (l, 0, 5)),
                  pl.BlockSpec((None, None, 1, d), lambda i, p: (l, TOKEN_TILE + jnp.minimum(i // tps, last), 0, 5)),
                  pl.BlockSpec(memory_space=pl.ANY)],
        out_specs=tile,
        scratch_shapes=[pltpu.VMEM((2, TOP_K, TOKEN_TILE, d), F32), pltpu.SemaphoreType.DMA((2,))],
    )
    return pl.pallas_call(
        functools.partial(_combine_kernel, ms.n_prompt_tiles),
        grid_spec=grid_spec,
        out_shape=jax.ShapeDtypeStruct((t, d), F32),
        compiler_params=_cparams(("arbitrary",)),
        name="moe_combine",
    )(pos.reshape(-1), x1, gates, mod, mod4, y)


def _permute_w_in(w_in, sizes):
    (fq, fk, fv, ff, dq, dk, dv, cq, ckv, kr) = sizes
    o = [0]
    for s in sizes:
        o.append(o[-1] + s)
    seg = lambda k: w_in[..., o[k]:o[k + 1]]
    pad = LANES - kr - ff
    parts = [seg(0), seg(4), seg(7), seg(1), seg(2), seg(5), seg(6), seg(8), seg(9), seg(3),
             jnp.zeros(w_in.shape[:-1] + (pad,), w_in.dtype)]
    off, acc = {}, 0
    for name, p in zip(("fq", "dq", "cq", "fkv", "_fv", "dk", "dv", "ckv", "tail", "_ff", "_pad"), parts):
        off[name] = acc
        acc += p.shape[-1]
    return jnp.concatenate(parts, axis=-1), off


def _rope_tables(positions):
    half = HEAD64 // 2
    inv_freq = ROPE_THETA ** (-jnp.arange(half, dtype=F32) / half)
    ang = positions.astype(F32)[:, None] * inv_freq[None, :]
    cos, sin = jnp.cos(ang), jnp.sin(ang)
    zero = jnp.zeros_like(sin)
    c = jnp.concatenate([cos] * 4, axis=-1)
    sa = jnp.concatenate([zero, sin, zero, sin], axis=-1)
    sb = jnp.concatenate([-sin, zero, -sin, zero], axis=-1)
    return c, sa, sb


def _indicator(n_heads, width):
    rows = jnp.arange(n_heads * width) // width
    return (rows[:, None] == jnp.arange(n_heads)[None, :]).astype(BF16)


def kernel(x_prompt, x_sample, cache_fox, cache_diff, cache_mla, page_table, c_prompt, c_sample, ada_w, ada_b,
           norm1_g, norm2_g, w_in, fox_fb, fox_qn, fox_kn, diff_qn, diff_kn, diff_lam, diff_subln, mla_cqn,
           mla_ckvn, mla_krn, mla_qn, mla_wuq, mla_wuk, mla_wuv, w_out, router_w, router_b, moe_w_gu, moe_b_gu,
           moe_w_dn, moe_b_dn):
    n_seq, seq, d = x_prompt.shape
    nb, dec_seq, _ = x_sample.shape
    depth = ada_w.shape[0]
    n_fox = fox_fb.shape[1]
    n_mla = mla_wuq.shape[2]
    q_rank = mla_wuq.shape[1]
    kv_rank = mla_wuk.shape[1]
    n_diff = (w_out.shape[1] - n_fox * HEAD64 - n_mla * LANES) // LANES
    n_exp = router_w.shape[-1]
    fox_row, diff_row, mla_row = cache_fox.shape[-1], cache_diff.shape[-1], cache_mla.shape[-1]
    n_pages = page_table.shape[1]
    past_len = n_pages * PAGE
    assert nb * dec_seq == TOKEN_TILE and dec_seq == 1 and kv_rank == LANES and cache_fox.shape[2] == PAGE
    assert seq % ATTN_TILE == 0 and n_pages % DECODE_PAGES == 0 and n_fox == n_mla == 2 * n_diff
    tp = n_seq * seq
    t = tp + TOKEN_TILE
    n_prompt_tiles = tp // TOKEN_TILE
    tiles_per_seq = seq // TOKEN_TILE

    sizes = (n_fox * HEAD64, HEAD64, HEAD64, n_fox, n_diff * LANES, LANES, LANES, q_rank, kv_rank, HEAD64)
    w_in_p, off = _permute_w_in(w_in, sizes)
    pos = jnp.concatenate([jnp.tile(jnp.arange(seq), n_seq), jnp.full((TOKEN_TILE,), past_len)])
    tabs = _rope_tables(pos)
    ln = jnp.arange(LANES)
    b64 = (ln[:, None] // HEAD64 == ln[None, :] // HEAD64).astype(BF16)
    tri_incl = (ln[None, :] <= ln[:, None]).astype(BF16)
    tri_strict = (ln[None, :] < ln[:, None]).astype(BF16)
    uext = jnp.concatenate([(ln[:, None] > ln[None, :]).astype(BF16), jnp.ones((LANES, LANES), BF16)], axis=1)
    indn, indr = _indicator(n_mla, MLA_NOPE), _indicator(n_mla, HEAD64)
    consts = (b64, tri_incl, indn, indn.T, indr, indr.T)
    wuq_p = jnp.concatenate([mla_wuq[..., :MLA_NOPE].reshape(depth, q_rank, -1),
                             mla_wuq[..., MLA_NOPE:].reshape(depth, q_rank, -1)], axis=-1)
    wuk_t = jnp.transpose(mla_wuk, (0, 2, 3, 1))
    wuv_h = jnp.transpose(mla_wuv, (0, 2, 1, 3)).astype(BF16)
    rw_hi = router_w.astype(BF16)
    rw_lo = (router_w - rw_hi.astype(F32)).astype(BF16)
    two = lambda g: jnp.concatenate([g, g], axis=-1)
    cache_fox_t = jnp.swapaxes(cache_fox, 2, 3)
    cache_mla_t = jnp.swapaxes(cache_mla, 2, 3)

    c_all =jnp.concatenate([c_sample, c_prompt], axis=0)
    mod = _ada_mod(c_all, ada_w, ada_b)
    mod4 = mod.reshape(depth, mod.shape[1], 1, mod.shape[2])

    x = jnp.concatenate([x_prompt.reshape(tp, d), x_sample.reshape(TOKEN_TILE, d)], axis=0)
    cfg = (n_prompt_tiles, tiles_per_seq, n_fox, n_diff, n_mla, q_rank, off)
    dims = dict(fox_row=fox_row, diff_row=diff_row, mla_row=mla_row)
    g_max = (t * TOP_K) // MOE_TILE + n_exp
    rows_out = []

    for l in range(depth):
        lam_init = 0.8 - 0.6 * math.exp(-0.3 * l)
        ms = _ModSpecs(l, d, n_prompt_tiles, tiles_per_seq, n_seq)
        h1 = _prenorm1(x, norm1_g[l][None], mod, mod4, ms)
        z = _matmul(h1, w_in_p, l, 640 if t % 640 == 0 else TOKEN_TILE, 1152 if w_in_p.shape[-1] % 1152 == 0 else LANES,
                    F32, "in_proj")
        gains = (two(fox_qn[l])[None], jnp.concatenate([fox_kn[l], jnp.ones((HEAD64,), F32)])[None],
                 two(diff_qn[l])[None], two(diff_kn[l])[None], mla_cqn[l][None], mla_ckvn[l][None],
                 two(mla_krn[l])[None], mla_qn[l][None, :MLA_NOPE], two(mla_qn[l][MLA_NOPE:])[None], fox_fb[l][None])
        pp = _prep(z, tabs, consts, gains, wuq_p[l], wuk_t[l], cfg, dims)
        rows_out.append((pp["frow"], pp["drow"], pp["mrow"]))

        nq = seq // ATTN_TILE
        cum_t = jnp.transpose(pp["cum"][:tp].reshape(n_seq, nq, ATTN_TILE, n_fox), (0, 1, 3, 2))
        of, od, om = _prompt_attention(pp, cum_t, diff_lam[l], diff_subln[l][None], wuv_h[l], lam_init,
                                       n_seq, seq, (n_fox, n_diff, n_mla))

        s_ = slice(tp, t)
        qf_s = pp["qf"][s_].reshape(nb, n_fox, HEAD64)
        qd_s = pp["qd"][s_].reshape(nb, n_diff, 2, HEAD64)
        zq = jnp.zeros_like(qd_s[:, :, 0])
        qd_s = jnp.concatenate([jnp.concatenate([qd_s[:, :, 0], zq], axis=-1),
                                jnp.concatenate([zq, qd_s[:, :, 1]], axis=-1)], axis=1)
        qm_s = jnp.concatenate([pp["qlat"][s_].reshape(nb, n_mla, LANES),
                                pp["qrope"][s_].reshape(nb, n_mla, HEAD64)], axis=-1)
        o_f, o_d, o_l = _decode_attention(l, page_table, qf_s, qd_s, qm_s,
                                          pp["frow"][s_][:, None], pp["drow"][s_][:, None], pp["mrow"][s_][:, None],
                                          pp["frow"][s_][:, LANES:, None], uext, cache_fox_t, cache_diff, cache_mla_t)
        o_s = _sample_merge(o_f.reshape(nb, -1), o_d[:, :n_diff].reshape(nb, -1),
                            o_d[:, n_diff:].reshape(nb, -1), o_l.reshape(nb, -1),
                            diff_lam[l], diff_subln[l][None], wuv_h[l], lam_init, d)

        o_all = jnp.concatenate([jnp.concatenate([of, od, om], axis=-1), o_s], axis=0)
        ao = _matmul(o_all, w_out, l, 640 if t % 640 == 0 else TOKEN_TILE, 1024, F32, "out_proj")

        x1, h2, top_i, gates, rank, counts = _prenorm2(x, ao, norm2_g[l][None], mod, mod4, ms,
                                                      rw_hi[l], rw_lo[l], router_b[l][None], tri_strict)

        cnt = counts[0].astype(I32)
        tiles_e = (cnt + MOE_TILE - 1) // MOE_TILE
        tile_end = jnp.cumsum(tiles_e)
        n_tiles = tile_end[-1]
        offs = (tile_end - tiles_e) * MOE_TILE
        pos_rows = offs[top_i] + rank
        gi = jnp.arange(g_max)
        tile_expert = jnp.minimum(jnp.sum((tile_end[None, :] <= jnp.minimum(gi, n_tiles - 1)[:, None]).astype(I32), axis=1),
                                  n_exp - 1)
        src = jnp.zeros((g_max * MOE_TILE,), I32).at[pos_rows.reshape(-1)].set(
            jnp.repeat(jnp.arange(t, dtype=I32), TOP_K))
        y = _moe_grouped(l, tile_expert, n_tiles.reshape(1).astype(I32), src, h2, moe_w_gu, moe_b_gu, moe_w_dn, moe_b_dn)
        x = _combine(pos_rows.astype(I32), x1, gates, y, mod, mod4, ms)

    def stack(k, lo, hi, shape):
        return jnp.stack([r[k][lo:hi].reshape(shape + (r[k].shape[-1],)) for r in rows_out])

    return (x[:tp].reshape(n_seq, seq, d), x[tp:].reshape(nb, dec_seq, d),
            stack(0, 0, tp, (n_seq, seq)), stack(1, 0, tp, (n_seq, seq)), stack(2, 0, tp, (n_seq, seq)),
            stack(0, tp, t, (nb, dec_seq)), stack(1, tp, t, (nb, dec_seq)), stack(2, tp, t, (nb, dec_seq)))
```
